```python
import math
import jax, jax.numpy as jnp
from jax import lax
import numpy as np

D_MODEL = 1024
BATCH = 16
SEQ = 2048
DEPTH = 4
DEC_BATCH = 16
DEC_SEQ = 4096
PAST_LEN = 128

N_MIXERS = 2
N_A_LAYERS = (DEPTH + N_MIXERS - 1) // N_MIXERS
N_B_LAYERS = DEPTH // N_MIXERS
HG_HEADS = 8
HG_DK = D_MODEL // HG_HEADS
HG_DV = D_MODEL // HG_HEADS
HG_WIDTH = HG_HEADS * HG_DK
HG_CHUNK = 64
DIL_PATTERNS = ((128, 1), (512, 4), (2048, 16))
N_GROUPS = len(DIL_PATTERNS)
ATT_HEADS = 16
ATT_HD = 64
ATT_WIDTH = ATT_HEADS * ATT_HD
BAND_BLOCK = max(w // (2 * d) for w, d in DIL_PATTERNS)
N_MEM = 256
X_HEADS = 4
X_HD = D_MODEL // X_HEADS
D_FF = 2816
N_NORMS = 9
EPS = 1e-6
NEG = -1e30

kernel_name = 'hgrn2_dilated_attn_hybrid_encoder'


def rmsnorm(x, g):
    x32 = x.astype(jnp.float32)
    y = x32 * lax.rsqrt(jnp.mean(x32 * x32, axis=-1, keepdims=True) + EPS)
    return (y * g.astype(jnp.float32)).astype(x.dtype)


def swiglu(h, w_in, w_out):
    gate, up = jnp.split(h @ w_in, 2, axis=-1)
    return (jax.nn.silu(gate) * up) @ w_out


def alibi_slopes(n):
    return jnp.exp2(-8.0 * jnp.arange(1, n + 1, dtype=jnp.float32) / n)


def forget_gate(z, lb):
    logf = jnp.logaddexp(jnp.log(lb), jnp.log1p(-lb) + jax.nn.log_sigmoid(z))
    k = (1.0 - lb) * jax.nn.sigmoid(-z)
    return logf, k


def _to_chunks(a):
    b, h, l, e = a.shape
    return a.reshape(b, h, l // HG_CHUNK, HG_CHUNK, e).transpose(2, 0, 1, 3, 4)


def hgrn_chunk_scan(q, k, v, logf):
    b, h, l, dk = q.shape
    dv = v.shape[-1]
    lower = jnp.tril(jnp.ones((HG_CHUNK, HG_CHUNK), dtype=bool))

    def step(S, xs):
        qc, kc, vc, lc = xs
        bcum = jnp.cumsum(lc, axis=2)
        btot = bcum[:, :, -1:, :]
        o_inter = jnp.einsum('bhtk,bhkv->bhtv', qc * jnp.exp(bcum), S)
        diff = bcum[:, :, :, None, :] - bcum[:, :, None, :, :]
        decay = jnp.exp(jnp.where(lower[:, :, None], diff, -jnp.inf))
        scores = jnp.einsum('bhtk,bhsk,bhtsk->bhts', qc, kc, decay)
        o_intra = jnp.einsum('bhts,bhsv->bhtv', scores, vc)
        S = jnp.exp(btot[:, :, 0, :])[..., None] * S + jnp.einsum(
            'bhsk,bhsv->bhkv', kc * jnp.exp(btot - bcum), vc)
        return S, o_inter + o_intra

    S0 = jnp.zeros((b, h, dk, dv), jnp.float32)
    _, o = lax.scan(step, S0, (_to_chunks(q), _to_chunks(k), _to_chunks(v), _to_chunks(logf)))
    return o.transpose(1, 2, 0, 3, 4).reshape(b, h, l, dv)


def hgrn2_mixer(h, w_in, lb, gnorm, w_out):
    b, l, _ = h.shape
    q, zf, zb, iv, g = jnp.split(h @ w_in, 5, axis=-1)

    def heads(a):
        return a.astype(jnp.float32).reshape(b, l, HG_HEADS, -1).transpose(0, 2, 1, 3)

    lbh = lb.reshape(HG_HEADS, 1, HG_DK)
    q = heads(jax.nn.silu(q))
    v = heads(iv)
    lf_f, k_f = forget_gate(heads(zf), lbh)
    lf_b, k_b = forget_gate(heads(zb), lbh)
    o_f = hgrn_chunk_scan(q, k_f, v, lf_f)
    rev = lambda a: jnp.flip(a, axis=2)
    o_b = rev(hgrn_chunk_scan(rev(q), rev(k_b), rev(v), rev(lf_b)))
    o = o_f + o_b
    o = o * lax.rsqrt(jnp.mean(o * o, axis=-1, keepdims=True) + EPS)
    o = o * gnorm.astype(jnp.float32) * jax.nn.silu(heads(g))
    o = o.transpose(0, 2, 1, 3).reshape(b, l, HG_WIDTH).astype(h.dtype)
    return o @ w_out


def banded_attention(q, k, v, radius, dilation, slopes):
    z, h, n, hd = q.shape
    nb = -(-n // BAND_BLOCK)
    npad = nb * BAND_BLOCK
    qb = jnp.pad(q, ((0, 0), (0, 0), (0, npad - n), (0, 0))).reshape(z, h, nb, BAND_BLOCK, hd)

    def windows(a):
        a = jnp.pad(a, ((0, 0), (0, 0), (BAND_BLOCK, npad - n + BAND_BLOCK), (0, 0)))
        a = a.reshape(z, h, nb + 2, BAND_BLOCK, hd)
        return jnp.concatenate([a[:, :, :-2], a[:, :, 1:-1], a[:, :, 2:]], axis=3)

    kw, vw = windows(k), windows(v)
    qpos = jnp.arange(npad).reshape(nb, BAND_BLOCK)
    kpos = (jnp.arange(nb) * BAND_BLOCK)[:, None] - BAND_BLOCK + jnp.arange(3 * BAND_BLOCK)[None, :]
    rel = jnp.abs(kpos[:, None, :] - qpos[:, :, None])
    valid = (rel <= radius) & (kpos[:, None, :] >= 0) & (kpos[:, None, :] < n)
    s = jnp.einsum('zhnqd,zhnkd->zhnqk', qb, kw).astype(jnp.float32) * (ATT_HD ** -0.5)
    s = s - slopes[:, None, None, None] * (dilation * rel).astype(jnp.float32)
    s = jnp.where(valid, s, NEG)
    lse = jax.nn.logsumexp(s, axis=-1)
    p = jnp.exp(s - lse[..., None])
    o = jnp.einsum('zhnqk,zhnkd->zhnqd', p, vw.astype(jnp.float32))
    return o.reshape(z, h, npad, hd)[:, :, :n], lse.reshape(z, h, npad)[:, :, :n]


def split_residues(a, dil):
    b, l = a.shape[:2]
    n = l // dil
    return a.reshape(b, n, dil, ATT_HEADS, ATT_HD).transpose(0, 2, 3, 1, 4).reshape(b * dil, ATT_HEADS, n, ATT_HD)


def dilated_attention_mixer(h, w_in, w_out):
    b, l, _ = h.shape
    proj = (h @ w_in).reshape(b, l, N_GROUPS, 3, ATT_HEADS, ATT_HD)
    slopes = alibi_slopes(ATT_HEADS)
    outs, lses = [], []
    for gi, (window, dil) in enumerate(DIL_PATTERNS):
        radius = window // (2 * dil)
        n = l // dil
        q = split_residues(proj[:, :, gi, 0], dil)
        k = split_residues(proj[:, :, gi, 1], dil)
        v = split_residues(proj[:, :, gi, 2], dil)
        o, lse = banded_attention(q, k, v, radius, dil, slopes)
        o = o.reshape(b, dil, ATT_HEADS, n, ATT_HD).transpose(0, 3, 1, 2, 4).reshape(b, l, ATT_HEADS, ATT_HD)
        lse = lse.reshape(b, dil, ATT_HEADS, n).transpose(0, 3, 1, 2).reshape(b, l, ATT_HEADS)
        outs.append(o)
        lses.append(lse)
    alpha = jax.nn.softmax(jnp.stack(lses), axis=0)
    o = jnp.sum(alpha[..., None] * jnp.stack(outs), axis=0)
    return o.reshape(b, l, ATT_WIDTH).astype(h.dtype) @ w_out


def memory_cross_attention(h, m, w_q, w_kv, w_o):
    b, l, _ = h.shape
    q = (h @ w_q).reshape(b, l, X_HEADS, X_HD)
    kv = (m @ w_kv).reshape(b, m.shape[1], 2, X_HEADS, X_HD)
    s = jnp.einsum('blhd,bmhd->bhlm', q, kv[:, :, 0]).astype(jnp.float32) * (X_HD ** -0.5)
    p = jax.nn.softmax(s, axis=-1)
    o = jnp.einsum('bhlm,bmhd->blhd', p, kv[:, :, 1].astype(jnp.float32))
    return o.reshape(b, l, D_MODEL).astype(h.dtype) @ w_o


def run_trunk(x, mem, norm_gains, ffn_w_in, ffn_w_out, hg_w_in, hg_lb_logits, hg_gnorm, hg_w_out,
              att_w_in, att_w_out, xa_w_q, xa_w_kv, xa_w_o):
    p = jax.nn.softmax(hg_lb_logits.astype(jnp.float32), axis=0)
    lower_bounds = jnp.maximum(jnp.cumsum(p, axis=0) - p[0], 0.0)
    for i in range(DEPTH):
        g = norm_gains[i]
        x = x + 0.5 * rmsnorm(swiglu(rmsnorm(x, g[0]), ffn_w_in[i, 0], ffn_w_out[i, 0]), g[1])
        h = rmsnorm(x, g[2])
        j = i // N_MIXERS
        if i % N_MIXERS == 0:
            t = hgrn2_mixer(h, hg_w_in[j], lower_bounds[i], hg_gnorm[j], hg_w_out[j])
        else:
            t = dilated_attention_mixer(h, att_w_in[j], att_w_out[j])
        x = x + rmsnorm(t, g[3])
        c = memory_cross_attention(rmsnorm(x, g[4]), rmsnorm(mem, g[5]), xa_w_q[i], xa_w_kv[i], xa_w_o[i])
        x = x + rmsnorm(c, g[6])
        x = x + 0.5 * rmsnorm(swiglu(rmsnorm(x, g[7]), ffn_w_in[i, 1], ffn_w_out[i, 1]), g[8])
    return x


def setup_inputs(seed: int = 0) -> dict:
    key = jax.random.key(seed)
    ks = jax.random.split(key, 16)
    f32 = jnp.float32

    def w(k, shape, fan_in):
        return jax.random.normal(k, shape, f32) * (fan_in ** -0.5)

    return {
        'x_prompt': jax.random.normal(ks[0], (BATCH, SEQ, D_MODEL), f32),
        'x_sample': jax.random.normal(ks[1], (DEC_BATCH, DEC_SEQ, D_MODEL), f32),
        'mem_prompt': jax.random.normal(ks[2], (BATCH, N_MEM, D_MODEL), f32),
        'mem_sample': jax.random.normal(ks[3], (DEC_BATCH, N_MEM, D_MODEL), f32),
        'norm_gains': 1.0 + 0.05 * jax.random.normal(ks[4], (DEPTH, N_NORMS, D_MODEL), f32),
        'ffn_w_in': w(ks[5], (DEPTH, 2, D_MODEL, 2 * D_FF), D_MODEL),
        'ffn_w_out': w(ks[6], (DEPTH, 2, D_FF, D_MODEL), D_FF),
        'hg_w_in': w(ks[7], (N_A_LAYERS, D_MODEL, 5 * HG_WIDTH), D_MODEL),
        'hg_lb_logits': 0.5 * jax.random.normal(ks[8], (DEPTH, HG_WIDTH), f32),
        'hg_gnorm': 1.0 + 0.05 * jax.random.normal(ks[9], (N_A_LAYERS, HG_DV), f32),
        'hg_w_out': w(ks[10], (N_A_LAYERS, HG_WIDTH, D_MODEL), HG_WIDTH),
        'att_w_in': w(ks[11], (N_B_LAYERS, D_MODEL, N_GROUPS * 3 * ATT_WIDTH), D_MODEL),
        'att_w_out': w(ks[12], (N_B_LAYERS, ATT_WIDTH, D_MODEL), ATT_WIDTH),
        'xa_w_q': w(ks[13], (DEPTH, D_MODEL, D_MODEL), D_MODEL),
        'xa_w_kv': w(ks[14], (DEPTH, D_MODEL, 2 * D_MODEL), D_MODEL),
        'xa_w_o': w(ks[15], (DEPTH, D_MODEL, D_MODEL), D_MODEL),
    }


def reference(x_prompt, x_sample, mem_prompt, mem_sample, norm_gains, ffn_w_in, ffn_w_out,
              hg_w_in, hg_lb_logits, hg_gnorm, hg_w_out, att_w_in, att_w_out, xa_w_q, xa_w_kv, xa_w_o):
    y_prompt = run_trunk(x_prompt, mem_prompt, norm_gains, ffn_w_in, ffn_w_out, hg_w_in, hg_lb_logits,
                         hg_gnorm, hg_w_out, att_w_in, att_w_out, xa_w_q, xa_w_kv, xa_w_o)
    y_sample = run_trunk(x_sample, mem_sample, norm_gains, ffn_w_in, ffn_w_out, hg_w_in, hg_lb_logits,
                         hg_gnorm, hg_w_out, att_w_in, att_w_out, xa_w_q, xa_w_kv, xa_w_o)
    return (y_prompt, y_sample)
```

```python
import functools

import jax
import jax.numpy as jnp
from jax import lax
from jax.experimental import pallas as pl
from jax.experimental.pallas import tpu as pltpu

F32 = jnp.float32
BF16 = jnp.bfloat16

D_MODEL = 1024
DEPTH = 4
HG_HEADS = 8
HG_DK = 128
DIL_PATTERNS = ((128, 1), (512, 4), (2048, 16))
ATT_HEADS = 16
ATT_HD = 64
ATT_RADIUS = 64
X_HEADS = 4
X_HD = 256
D_FF = 2816
EPS = 1e-6
NEG = -1e30

HG_CHUNK = 64
FF_CHUNK = 256
ATT_QBLOCK = 128
VMEM_LIMIT = 56 * 1024 * 1024

_NT = (((1,), (1,)), ((), ()))
_TN = (((0,), (0,)), ((), ()))


def _params(*sem):
    return pltpu.CompilerParams(dimension_semantics=sem, vmem_limit_bytes=VMEM_LIMIT)


def _rms(x, g):
    return x * lax.rsqrt(jnp.mean(x * x, axis=-1, keepdims=True) + EPS) * g


def _silu(x):
    return x * jax.nn.sigmoid(x)


def _row_tile(t, pref):
    while t % pref:
        pref //= 2
    return pref


def _rms_matmul_kernel(x_ref, g_ref, w_ref, o_ref, h_ref):
    @pl.when(pl.program_id(1) == 0)
    def _():
        h_ref[...] = _rms(x_ref[...], g_ref[...]).astype(BF16)

    o_ref[...] = jnp.dot(h_ref[...], w_ref[...], preferred_element_type=F32).astype(o_ref.dtype)


def rms_matmul(x, g, w, out_dtype, tm=1024, tn=512):
    t, d = x.shape
    n = w.shape[1]
    tm = _row_tile(t, tm)
    tn = _row_tile(n, tn)
    return pl.pallas_call(
        _rms_matmul_kernel,
        out_shape=jax.ShapeDtypeStruct((t, n), out_dtype),
        grid=(t // tm, n // tn),
        in_specs=[
            pl.BlockSpec((tm, d), lambda i, j: (i, 0)),
            pl.BlockSpec((1, d), lambda i, j: (0, 0)),
            pl.BlockSpec((d, tn), lambda i, j: (0, j)),
        ],
        out_specs=pl.BlockSpec((tm, tn), lambda i, j: (i, j)),
        scratch_shapes=[pltpu.VMEM((tm, d), BF16)],
        compiler_params=_params("parallel", "arbitrary"),
        name="rms_matmul",
    )(x, g, w)


def _ffn_kernel(x_ref, g0_ref, g1_ref, wg_ref, wu_ref, wo_ref, o_ref, h_ref, acc_ref):
    c = pl.program_id(1)

    @pl.when(c == 0)
    def _():
        h_ref[...] = _rms(x_ref[...], g0_ref[...]).astype(BF16)
        acc_ref[...] = jnp.zeros_like(acc_ref)

    h = h_ref[...]
    gate = jnp.dot(h, wg_ref[...], preferred_element_type=F32)
    up = jnp.dot(h, wu_ref[...], preferred_element_type=F32)
    act = (_silu(gate) * up).astype(BF16)
    acc_ref[...] += jnp.dot(act, wo_ref[...], preferred_element_type=F32)

    @pl.when(c == pl.num_programs(1) - 1)
    def _():
        o_ref[...] = x_ref[...] + 0.5 * _rms(acc_ref[...], g1_ref[...])


def ffn_block(x, g0, g1, w_in, w_out, tm=512):
    t, d = x.shape
    tm = _row_tile(t, tm)
    nf = D_FF // FF_CHUNK
    return pl.pallas_call(
        _ffn_kernel,
        out_shape=jax.ShapeDtypeStruct((t, d), F32),
        grid=(t // tm, nf),
        in_specs=[
            pl.BlockSpec((tm, d), lambda i, c: (i, 0)),
            pl.BlockSpec((1, d), lambda i, c: (0, 0)),
            pl.BlockSpec((1, d), lambda i, c: (0, 0)),
            pl.BlockSpec((d, FF_CHUNK), lambda i, c: (0, c)),
            pl.BlockSpec((d, FF_CHUNK), lambda i, c: (0, c + D_FF // FF_CHUNK)),
            pl.BlockSpec((FF_CHUNK, d), lambda i, c: (c, 0)),
        ],
        out_specs=pl.BlockSpec((tm, d), lambda i, c: (i, 0)),
        scratch_shapes=[pltpu.VMEM((tm, d), BF16), pltpu.VMEM((tm, d), F32)],
        compiler_params=_params("parallel", "arbitrary"),
        name="ffn_block",
    )(x, g0, g1, w_in, w_in, w_out)


def _outproj_kernel(t_ref, w_ref, g_ref, x_ref, o_ref):
    y = jnp.dot(t_ref[...], w_ref[...], preferred_element_type=F32)
    o_ref[...] = x_ref[...] + _rms(y, g_ref[...])


def outproj_block(t_in, w, g, x, tm=512):
    t, d = x.shape
    tm = _row_tile(t, tm)
    row = pl.BlockSpec((tm, d), lambda i: (i, 0))
    return pl.pallas_call(
        _outproj_kernel,
        out_shape=jax.ShapeDtypeStruct((t, d), F32),
        grid=(t // tm,),
        in_specs=[row, pl.BlockSpec((d, d), lambda i: (0, 0)), pl.BlockSpec((1, d), lambda i: (0, 0)), row],
        out_specs=row,
        compiler_params=_params("parallel"),
        name="outproj_block",
    )(t_in, w, g, x)


def _merge_outproj_kernel(o0_ref, o1_ref, o2_ref, l0_ref, l1_ref, l2_ref, w_ref, g_ref, x_ref, o_ref):
    l0, l1, l2 = l0_ref[...], l1_ref[...], l2_ref[...]
    m = jnp.maximum(jnp.maximum(l0, l1), l2)
    e0, e1, e2 = jnp.exp(l0 - m), jnp.exp(l1 - m), jnp.exp(l2 - m)
    num = e0 * o0_ref[...].astype(F32) + e1 * o1_ref[...].astype(F32) + e2 * o2_ref[...].astype(F32)
    merged = (num / (e0 + e1 + e2)).astype(BF16)
    y = jnp.dot(merged, w_ref[...], preferred_element_type=F32)
    o_ref[...] = x_ref[...] + _rms(y, g_ref[...])


def merge_outproj_block(outs, lses, w, g, x, tm=256):
    t, d = x.shape
    tm = _row_tile(t, tm)
    row = pl.BlockSpec((tm, d), lambda i: (i, 0))
    return pl.pallas_call(
        _merge_outproj_kernel,
        out_shape=jax.ShapeDtypeStruct((t, d), F32),
        grid=(t // tm,),
        in_specs=[row] * 6 + [pl.BlockSpec((d, d), lambda i: (0, 0)), pl.BlockSpec((1, d), lambda i: (0, 0)), row],
        out_specs=row,
        compiler_params=_params("parallel"),
        name="merge_outproj_block",
    )(*outs, *lses, w, g, x)


def _xattn_kernel(x_ref, k_ref, v_ref, wq_ref, wo_ref, gq_ref, go_ref, o_ref):
    x = x_ref[0]
    q = jnp.dot(_rms(x, gq_ref[...]).astype(BF16), wq_ref[...], preferred_element_type=F32).astype(BF16)
    k = k_ref[0]
    v = v_ref[0]
    heads = []
    for h in range(X_HEADS):
        sl = slice(h * X_HD, (h + 1) * X_HD)
        s = lax.dot_general(q[:, sl], k[:, sl], _NT, preferred_element_type=F32) * (X_HD ** -0.5)
        p = jnp.exp(s - jnp.max(s, axis=-1, keepdims=True))
        denom = jnp.sum(p, axis=-1, keepdims=True)
        heads.append(jnp.dot(p.astype(BF16), v[:, sl], preferred_element_type=F32) / denom)
    o = jnp.concatenate(heads, axis=-1).astype(BF16)
    c = jnp.dot(o, wo_ref[...], preferred_element_type=F32)
    o_ref[0] = x + _rms(c, go_ref[...])


def xattn_block(x, kv, w_q, w_o, g_q, g_o, tm=512):
    b, l, d = x.shape
    m = kv.shape[1]
    tm = _row_tile(l, tm)
    row = pl.BlockSpec((1, tm, d), lambda i, j: (i, j, 0))
    full = pl.BlockSpec((d, d), lambda i, j: (0, 0))
    gain = pl.BlockSpec((1, d), lambda i, j: (0, 0))
    return pl.pallas_call(
        _xattn_kernel,
        out_shape=jax.ShapeDtypeStruct((b, l, d), F32),
        grid=(b, l // tm),
        in_specs=[
            row,
            pl.BlockSpec((1, m, d), lambda i, j: (i, 0, 0)),
            pl.BlockSpec((1, m, d), lambda i, j: (i, 0, 1)),
            full, full, gain, gain,
        ],
        out_specs=row,
        compiler_params=_params("parallel", "parallel"),
        name="xattn_block",
    )(x, kv, kv, w_q, w_o, g_q, g_o)


def _hgrn_kernel(q_ref, zf_ref, zb_ref, v_ref, g_ref, lb_ref, gn_ref, o_ref, of_ref, st_ref, *, seq):
    c_len = HG_CHUNK
    nc = seq // c_len
    half = c_len // 2
    lb = lb_ref[0]
    log_lb = jnp.log(lb)
    log_1m_lb = jnp.log1p(-lb)
    row = lax.broadcasted_iota(jnp.int32, (c_len, c_len), 0)
    col = lax.broadcasted_iota(jnp.int32, (c_len, c_len), 1)
    lower = col <= row
    upper = col >= row

    def gate(z):
        log_sig = jnp.minimum(z, 0.0) - jnp.log1p(jnp.exp(-jnp.abs(z)))
        t = log_1m_lb + log_sig
        logf = jnp.maximum(log_lb, t) + jnp.log1p(jnp.exp(-jnp.abs(log_lb - t)))
        return logf, (1.0 - lb) * jax.nn.sigmoid(-z)

    def chunk(c, z_ref, causal, ref_row, tot_row):
        sl = pl.ds(pl.multiple_of(c * c_len, c_len), c_len)
        q = _silu(q_ref[0, sl, :])
        v = v_ref[0, sl, :].astype(BF16)
        logf, k = gate(z_ref[0, sl, :])
        cum = jnp.dot(causal.astype(F32), logf, precision=lax.Precision.HIGHEST, preferred_element_type=F32)
        mid = cum[ref_row:ref_row + 1, :]
        tot = cum[tot_row:tot_row + 1, :]
        st = st_ref[...]
        o = lax.dot_general((q * jnp.exp(cum)).astype(BF16), st.astype(BF16), _NT, preferred_element_type=F32)
        qm = (q * jnp.exp(cum - mid)).astype(BF16)
        km = (k * jnp.exp(mid - cum)).astype(BF16)
        s = lax.dot_general(qm, km, _NT, preferred_element_type=F32)
        s = jnp.where(causal, s, 0.0).astype(BF16)
        o = o + jnp.dot(s, v, preferred_element_type=F32)
        ke = (k * jnp.exp(tot - cum)).astype(BF16)
        st_ref[...] = st * jnp.exp(tot) + lax.dot_general(v, ke, _TN, preferred_element_type=F32)
        return sl, o

    st_ref[...] = jnp.zeros_like(st_ref)

    def fwd(c, carry):
        sl, o = chunk(c, zf_ref, lower, half - 1, c_len - 1)
        of_ref[sl, :] = o
        return carry

    lax.fori_loop(0, nc, fwd, 0)
    st_ref[...] = jnp.zeros_like(st_ref)

    def bwd(i, carry):
        sl, o = chunk(nc - 1 - i, zb_ref, upper, half, 0)
        o = o + of_ref[sl, :]
        o = o * lax.rsqrt(jnp.mean(o * o, axis=-1, keepdims=True) + EPS)
        o_ref[0, sl, :] = (o * gn_ref[...] * _silu(g_ref[0, sl, :])).astype(o_ref.dtype)
        return carry

    lax.fori_loop(0, nc, bwd, 0)


def hgrn_scan(proj, lb, gnorm):
    b, l, _ = proj.shape
    part = lambda p: pl.BlockSpec((1, l, HG_DK), lambda i, h, p=p: (i, 0, p * HG_HEADS + h))
    return pl.pallas_call(
        functools.partial(_hgrn_kernel, seq=l),
        out_shape=jax.ShapeDtypeStruct((b, l, D_MODEL), BF16),
        grid=(b, HG_HEADS),
        in_specs=[part(0), part(1), part(2), part(3), part(4),
                  pl.BlockSpec((1, 1, HG_DK), lambda i, h: (h, 0, 0)),
                  pl.BlockSpec((1, HG_DK), lambda i, h: (0, 0))],
        out_specs=pl.BlockSpec((1, l, HG_DK), lambda i, h: (i, 0, h)),
        scratch_shapes=[pltpu.VMEM((l, HG_DK), F32), pltpu.VMEM((HG_DK, HG_DK), F32)],
        compiler_params=_params("parallel", "parallel"),
        name="hgrn_scan",
    )(proj, proj, proj, proj, proj, lb, gnorm)


def _band_kernel(slope_ref, q_ref, k_ref, v_ref, o_ref, l_ref, *, n, bq, win, dil):
    pair = pl.program_id(1)

    def body(i, carry):
        q0 = pl.multiple_of(i * bq, bq)
        k0 = pl.multiple_of(jnp.clip(q0 - ATT_RADIUS, 0, n - win), 16)
        q = q_ref[0, 0, pl.ds(q0, bq), :]
        k = k_ref[0, 0, pl.ds(k0, win), :]
        v = v_ref[0, 0, pl.ds(k0, win), :]
        qpos = q0 + lax.broadcasted_iota(jnp.int32, (bq, win), 0)
        kpos = k0 + lax.broadcasted_iota(jnp.int32, (bq, win), 1)
        rel = jnp.abs(kpos - qpos)
        valid = rel <= ATT_RADIUS
        dist = (dil * rel).astype(F32)
        outs, lses = [], []
        for h in range(2):
            sl = slice(h * ATT_HD, (h + 1) * ATT_HD)
            s = lax.dot_general(q[:, sl], k[:, sl], _NT, preferred_element_type=F32) * (ATT_HD ** -0.5)
            s = jnp.where(valid, s - slope_ref[2 * pair + h] * dist, NEG)
            m = jnp.max(s, axis=-1, keepdims=True)
            p = jnp.exp(s - m)
            denom = jnp.sum(p, axis=-1, keepdims=True)
            outs.append(jnp.dot(p.astype(BF16), v[:, sl], preferred_element_type=F32) / denom)
            lses.append(jnp.broadcast_to(m + jnp.log(denom), (bq, ATT_HD)))
        o_ref[0, 0, pl.ds(q0, bq), :] = jnp.concatenate(outs, axis=-1).astype(o_ref.dtype)
        l_ref[0, 0, pl.ds(q0, bq), :] = jnp.concatenate(lses, axis=-1)
        return carry

    lax.fori_loop(0, n // bq, body, 0)


def band_attention(slopes, q, k, v, dil):
    z, hp, n, w = q.shape
    bq = min(ATT_QBLOCK, n)
    win = min(bq + 2 * ATT_RADIUS, n)
    blk = pl.BlockSpec((1, 1, n, w), lambda i, j: (i, j, 0, 0))
    return pl.pallas_call(
        functools.partial(_band_kernel, n=n, bq=bq, win=win, dil=dil),
        out_shape=(jax.ShapeDtypeStruct(q.shape, BF16), jax.ShapeDtypeStruct(q.shape, F32)),
        grid=(z, hp),
        in_specs=[pl.BlockSpec(memory_space=pltpu.SMEM), blk, blk, blk],
        out_specs=(blk, blk),
        compiler_params=_params("parallel", "parallel"),
        name="band_attention",
    )(slopes, q, k, v)


def _split_residues(a, dil):
    b, l, _ = a.shape
    n = l // dil
    return a.reshape(b, n, dil, 8, 128).transpose(0, 2, 3, 1, 4).reshape(b * dil, 8, n, 128)


def _merge_residues(a, b, dil):
    n = a.shape[2]
    return a.reshape(b, dil, 8, n, 128).transpose(0, 3, 1, 2, 4).reshape(b * n * dil, D_MODEL)


def _trunk(x, mem, p):
    b, l, d = x.shape
    t = b * l
    x = x.reshape(t, d)
    memf = mem.reshape(b * mem.shape[1], d)
    for i in range(DEPTH):
        g = p["gains"][i]
        x = ffn_block(x, g[0], g[1], p["ffn_w_in"][i, 0], p["ffn_w_out"][i, 0])
        j = i // 2
        if i % 2 == 0:
            proj = rms_matmul(x, g[2], p["hg_w_in"][j], F32)
            mix = hgrn_scan(proj.reshape(b, l, -1), p["lower_bounds"][i], p["hg_gnorm"][j])
            x = outproj_block(mix.reshape(t, d), p["hg_w_out"][j], g[3], x)
        else:
            proj = rms_matmul(x, g[2], p["att_w_in"][j], BF16).reshape(b, l, -1)
            outs, lses = [], []
            for gi, (_, dil) in enumerate(DIL_PATTERNS):
                q, k, v = (_split_residues(proj[:, :, (3 * gi + c) * d:(3 * gi + c + 1) * d], dil) for c in range(3))
                o, lse = band_attention(p["slopes"], q, k, v, dil)
                outs.append(_merge_residues(o, b, dil))
                lses.append(_merge_residues(lse, b, dil))
            x = merge_outproj_block(outs, lses, p["att_w_out"][j], g[3], x)
        kv = rms_matmul(memf, g[5], p["xa_w_kv"][i], BF16).reshape(b, -1, 2 * d)
        x = xattn_block(x.reshape(b, l, d), kv, p["xa_w_q"][i], p["xa_w_o"][i], g[4], g[6]).reshape(t, d)
        x = ffn_block(x, g[7], g[8], p["ffn_w_in"][i, 1], p["ffn_w_out"][i, 1])
    return x.reshape(b, l, d)


def kernel(x_prompt, x_sample, mem_prompt, mem_sample, norm_gains, ffn_w_in, ffn_w_out, hg_w_in, hg_lb_logits, hg_gnorm, hg_w_out, att_w_in, att_w_out, xa_w_q, xa_w_kv, xa_w_o):
    sm = jax.nn.softmax(hg_lb_logits.astype(F32), axis=0)
    lower_bounds = jnp.maximum(jnp.cumsum(sm, axis=0) - sm[0], 0.0)
    p = {
        "gains": norm_gains.astype(F32).reshape(DEPTH, -1, 1, D_MODEL),
        "ffn_w_in": ffn_w_in.astype(BF16),
        "ffn_w_out": ffn_w_out.astype(BF16),
        "hg_w_in": hg_w_in.astype(BF16),
        "lower_bounds": lower_bounds.reshape(DEPTH, HG_HEADS, 1, HG_DK),
        "hg_gnorm": hg_gnorm.astype(F32).reshape(-1, 1, HG_DK),
        "hg_w_out": hg_w_out.astype(BF16),
        "att_w_in": att_w_in.astype(BF16),
        "att_w_out": att_w_out.astype(BF16),
        "xa_w_q": xa_w_q.astype(BF16),
        "xa_w_kv": xa_w_kv.astype(BF16),
        "xa_w_o": xa_w_o.astype(BF16),
        "slopes": jnp.exp2(-8.0 * jnp.arange(1, ATT_HEADS + 1, dtype=F32) / ATT_HEADS),
    }
    return (_trunk(x_prompt, mem_prompt, p), _trunk(x_sample, mem_sample, p))
```

```python
import functools

import jax
import jax.numpy as jnp
from jax import lax
from jax.experimental import pallas as pl
from jax.experimental.pallas import tpu as pltpu

F32 = jnp.float32
BF16 = jnp.bfloat16

D_MODEL = 1024
LANES = 128
N_SLABS = D_MODEL // LANES
DEPTH = 4
HG_HEADS = 8
HG_DK = 128
DIL_PATTERNS = ((128, 1), (512, 4), (2048, 16))
ATT_HEADS = 16
ATT_HD = 64
ATT_RADIUS = 64
X_HEADS = 4
X_HD = 256
D_FF = 2816
EPS = 1e-6
NEG = -1e30

HG_CHUNK = 64
HG_GROUP = 8
FF_CHUNK = 256
ATT_QBLOCK = 128
ATT_ROWS_PER_STEP = 4096
VMEM_LIMIT = 56 * 1024 * 1024

_NT = (((1,), (1,)), ((), ()))
_TN = (((0,), (0,)), ((), ()))


def _params(*sem):
    return pltpu.CompilerParams(dimension_semantics=sem, vmem_limit_bytes=VMEM_LIMIT)


def _rms(x, g):
    return x * lax.rsqrt(jnp.mean(x * x, axis=-1, keepdims=True) + EPS) * g


def _silu(x):
    return x / (1.0 + jnp.exp(-x))


def _row_tile(t, pref):
    while t % pref:
        pref //= 2
    return pref


def _slabs(ref):
    return jnp.concatenate([ref[0, c] for c in range(ref.shape[1])], axis=-1)


def _rms_matmul_kernel(x_ref, g_ref, w_ref, o_ref, h_ref, *, slabbed):
    @pl.when(pl.program_id(2) == 0)
    def _():
        h_ref[...] = _rms(x_ref[0], g_ref[...]).astype(BF16)

    y = jnp.dot(h_ref[...], w_ref[...], preferred_element_type=F32).astype(o_ref.dtype)
    if slabbed:
        for c in range(o_ref.shape[1]):
            o_ref[0, c] = y[:, c * LANES:(c + 1) * LANES]
    else:
        o_ref[0] = y


def rms_matmul(x, g, w, out_dtype, slabbed, tm=1024, tn=512):
    b, l, d = x.shape
    n = w.shape[1]
    tm = _row_tile(l, tm)
    tn = _row_tile(n, tn)
    if slabbed:
        out_shape = (b, n // LANES, l, LANES)
        out_spec = pl.BlockSpec((1, tn // LANES, tm, LANES), lambda bi, i, j: (bi, j, i, 0))
    else:
        out_shape = (b, l, n)
        out_spec = pl.BlockSpec((1, tm, tn), lambda bi, i, j: (bi, i, j))
    return pl.pallas_call(
        functools.partial(_rms_matmul_kernel, slabbed=slabbed),
        out_shape=jax.ShapeDtypeStruct(out_shape, out_dtype),
        grid=(b, l // tm, n // tn),
        in_specs=[
            pl.BlockSpec((1, tm, d), lambda bi, i, j: (bi, i, 0)),
            pl.BlockSpec((1, d), lambda bi, i, j: (0, 0)),
            pl.BlockSpec((d, tn), lambda bi, i, j: (0, j)),
        ],
        out_specs=out_spec,
        scratch_shapes=[pltpu.VMEM((tm, d), BF16)],
        compiler_params=_params("parallel", "parallel", "arbitrary"),
        name="rms_matmul",
    )(x, g, w)


def _ffn_kernel(x_ref, g0_ref, g1_ref, wg_ref, wu_ref, wo_ref, o_ref, h_ref, acc_ref):
    c = pl.program_id(1)

    @pl.when(c == 0)
    def _():
        h_ref[...] = _rms(x_ref[...], g0_ref[...]).astype(BF16)
        acc_ref[...] = jnp.zeros_like(acc_ref)

    h = h_ref[...]
    gate = jnp.dot(h, wg_ref[...], preferred_element_type=F32)
    up = jnp.dot(h, wu_ref[...], preferred_element_type=F32)
    act = (_silu(gate) * up).astype(BF16)
    acc_ref[...] += jnp.dot(act, wo_ref[...], preferred_element_type=F32)

    @pl.when(c == pl.num_programs(1) - 1)
    def _():
        o_ref[...] = x_ref[...] + 0.5 * _rms(acc_ref[...], g1_ref[...])


def ffn_block(x, g0, g1, w_in, w_out, tm=512):
    t, d = x.shape
    tm = _row_tile(t, tm)
    nf = D_FF // FF_CHUNK
    return pl.pallas_call(
        _ffn_kernel,
        out_shape=jax.ShapeDtypeStruct((t, d), F32),
        grid=(t // tm, nf),
        in_specs=[
            pl.BlockSpec((tm, d), lambda i, c: (i, 0)),
            pl.BlockSpec((1, d), lambda i, c: (0, 0)),
            pl.BlockSpec((1, d), lambda i, c: (0, 0)),
            pl.BlockSpec((d, FF_CHUNK), lambda i, c: (0, c)),
            pl.BlockSpec((d, FF_CHUNK), lambda i, c: (0, c + D_FF // FF_CHUNK)),
            pl.BlockSpec((FF_CHUNK, d), lambda i, c: (c, 0)),
        ],
        out_specs=pl.BlockSpec((tm, d), lambda i, c: (i, 0)),
        scratch_shapes=[pltpu.VMEM((tm, d), BF16), pltpu.VMEM((tm, d), F32)],
        compiler_params=_params("parallel", "arbitrary"),
        name="ffn_block",
    )(x, g0, g1, w_in, w_in, w_out)


def _outproj_kernel(t_ref, w_ref, g_ref, x_ref, o_ref):
    y = jnp.dot(_slabs(t_ref), w_ref[...], preferred_element_type=F32)
    o_ref[0] = x_ref[0] + _rms(y, g_ref[...])


def _merge_outproj_kernel(o0_ref, o1_ref, o2_ref, l0_ref, l1_ref, l2_ref, w_ref, g_ref, x_ref, o_ref):
    l0, l1, l2 = _slabs(l0_ref), _slabs(l1_ref), _slabs(l2_ref)
    m = jnp.maximum(jnp.maximum(l0, l1), l2)
    e0, e1, e2 = jnp.exp(l0 - m), jnp.exp(l1 - m), jnp.exp(l2 - m)
    num = e0 * _slabs(o0_ref).astype(F32) + e1 * _slabs(o1_ref).astype(F32) + e2 * _slabs(o2_ref).astype(F32)
    merged = (num / (e0 + e1 + e2)).astype(BF16)
    y = jnp.dot(merged, w_ref[...], preferred_element_type=F32)
    o_ref[0] = x_ref[0] + _rms(y, g_ref[...])


def outproj_block(slab_inputs, w, g, x, tm):
    b, l, d = x.shape
    tm = _row_tile(l, tm)
    row = pl.BlockSpec((1, tm, d), lambda bi, i: (bi, i, 0))
    slab = pl.BlockSpec((1, N_SLABS, tm, LANES), lambda bi, i: (bi, 0, i, 0))
    body = _outproj_kernel if len(slab_inputs) == 1 else _merge_outproj_kernel
    return pl.pallas_call(
        body,
        out_shape=jax.ShapeDtypeStruct((b, l, d), F32),
        grid=(b, l // tm),
        in_specs=[slab] * len(slab_inputs)
        + [pl.BlockSpec((d, d), lambda bi, i: (0, 0)), pl.BlockSpec((1, d), lambda bi, i: (0, 0)), row],
        out_specs=row,
        compiler_params=_params("parallel", "parallel"),
        name="outproj_block",
    )(*slab_inputs, w, g, x)


def _xattn_kernel(x_ref, k_ref, v_ref, wq_ref, wo_ref, gq_ref, go_ref, o_ref):
    x = x_ref[0]
    q = jnp.dot(_rms(x, gq_ref[...]).astype(BF16), wq_ref[...], preferred_element_type=F32).astype(BF16)
    k = k_ref[0]
    v = v_ref[0]
    heads = []
    for h in range(X_HEADS):
        sl = slice(h * X_HD, (h + 1) * X_HD)
        s = lax.dot_general(q[:, sl], k[:, sl], _NT, preferred_element_type=F32) * (X_HD ** -0.5)
        p = jnp.exp(s - jnp.max(s, axis=-1, keepdims=True))
        denom = jnp.sum(p, axis=-1, keepdims=True)
        heads.append(jnp.dot(p.astype(BF16), v[:, sl], preferred_element_type=F32) / denom)
    o = jnp.concatenate(heads, axis=-1).astype(BF16)
    c = jnp.dot(o, wo_ref[...], preferred_element_type=F32)
    o_ref[0] = x + _rms(c, go_ref[...])


def xattn_block(x, kv, w_q, w_o, g_q, g_o, tm=512):
    b, l, d = x.shape
    m = kv.shape[1]
    tm = _row_tile(l, tm)
    row = pl.BlockSpec((1, tm, d), lambda i, j: (i, j, 0))
    full = pl.BlockSpec((d, d), lambda i, j: (0, 0))
    gain = pl.BlockSpec((1, d), lambda i, j: (0, 0))
    return pl.pallas_call(
        _xattn_kernel,
        out_shape=jax.ShapeDtypeStruct((b, l, d), F32),
        grid=(b, l // tm),
        in_specs=[
            row,
            pl.BlockSpec((1, m, d), lambda i, j: (i, 0, 0)),
            pl.BlockSpec((1, m, d), lambda i, j: (i, 0, 1)),
            full, full, gain, gain,
        ],
        out_specs=row,
        compiler_params=_params("parallel", "parallel"),
        name="xattn_block",
    )(x, kv, kv, w_q, w_o, g_q, g_o)


def _hgrn_kernel(q_ref, zf_ref, zb_ref, v_ref, g_ref, lb_ref, gn_ref, o_ref, of_ref, ob_ref, stf_ref, stb_ref, *, seq):
    c_len = HG_CHUNK
    nc = seq // c_len
    half = c_len // 2
    lb = lb_ref[0]
    log_lb = jnp.log(lb)
    log_1m_lb = jnp.log1p(-lb)
    row = lax.broadcasted_iota(jnp.int32, (c_len, c_len), 0)
    col = lax.broadcasted_iota(jnp.int32, (c_len, c_len), 1)
    lower = col <= row
    upper = col >= row

    def gate(z):
        e = jnp.exp(-jnp.abs(z))
        log_sig = jnp.minimum(z, 0.0) - jnp.log(1.0 + e)
        t = log_1m_lb + log_sig
        logf = jnp.maximum(log_lb, t) + jnp.log(1.0 + jnp.exp(-jnp.abs(log_lb - t)))
        sig_neg = jnp.where(z >= 0.0, e, 1.0) / (1.0 + e)
        return logf, (1.0 - lb) * sig_neg

    grp = _row_tile(nc, HG_GROUP)
    ng = nc // grp
    scans = ((zf_ref, stf_ref, of_ref, lower, half - 1, c_len - 1, False),
             (zb_ref, stb_ref, ob_ref, upper, half, 0, True))

    stf_ref[...] = jnp.zeros_like(stf_ref)
    stb_ref[...] = jnp.zeros_like(stb_ref)

    def scan(i, carry):
        steps = []
        for z_ref, st_ref, out_ref, causal, ref_row, tot_row, rev in scans:
            g0 = (ng - 1 - i) if rev else i
            sls = [pl.ds(pl.multiple_of((g0 * grp + j) * c_len, c_len), c_len) for j in range(grp)]
            gates = [gate(z_ref[0, 0, sl, :]) for sl in sls]
            cums = jnp.dot(causal.astype(F32), jnp.concatenate([lf for lf, _ in gates], axis=1),
                           precision=lax.Precision.HIGHEST, preferred_element_type=F32)
            for j in (reversed(range(grp)) if rev else range(grp)):
                steps.append(dict(sl=sls[j], k=gates[j][1], cum=cums[:, j * HG_DK:(j + 1) * HG_DK], causal=causal,
                                  ref_row=ref_row, tot_row=tot_row, st_ref=st_ref, out_ref=out_ref))
        order = [steps[d * grp + j] for j in range(grp) for d in range(2)]
        for e in order:
            cum = e["cum"]
            mid = cum[e["ref_row"]:e["ref_row"] + 1, :]
            tot = cum[e["tot_row"]:e["tot_row"] + 1, :]
            q = _silu(q_ref[0, 0, e["sl"], :])
            e["v"] = v_ref[0, 0, e["sl"], :].astype(BF16)
            qm = q * jnp.exp(cum - mid)
            km = e["k"] * jnp.exp(mid - cum)
            e["qe"] = (qm * jnp.exp(mid)).astype(BF16)
            ke = (km * jnp.exp(tot - mid)).astype(BF16)
            e["decay"] = jnp.exp(tot)
            e["s"] = lax.dot_general(qm.astype(BF16), km.astype(BF16), _NT, preferred_element_type=F32)
            e["upd"] = lax.dot_general(e["v"], ke, _TN, preferred_element_type=F32)
        for e in order:
            s = jnp.where(e["causal"], e["s"], 0.0).astype(BF16)
            e["o"] = jnp.dot(s, e["v"], preferred_element_type=F32)
        states = [st_ref[...] for _, st_ref, *_ in scans]
        for idx, e in enumerate(order):
            st = states[idx % 2]
            o = e["o"] + lax.dot_general(e["qe"], st.astype(BF16), _NT, preferred_element_type=F32)
            e["out_ref"][e["sl"], :] = o
            states[idx % 2] = st * e["decay"] + e["upd"]
        for st, (_, st_ref, *_) in zip(states, scans):
            st_ref[...] = st
        return carry

    lax.fori_loop(0, ng, scan, 0)

    rows = _row_tile(seq, 256)

    def finish(i, carry):
        sl = pl.ds(pl.multiple_of(i * rows, rows), rows)
        o = of_ref[sl, :] + ob_ref[sl, :]
        o = o * lax.rsqrt(jnp.mean(o * o, axis=-1, keepdims=True) + EPS)
        o_ref[0, 0, sl, :] = (o * gn_ref[...] * _silu(g_ref[0, 0, sl, :])).astype(o_ref.dtype)
        return carry

    lax.fori_loop(0, seq // rows, finish, 0)


def hgrn_scan(proj, lb, gnorm):
    b, _, l, _ = proj.shape
    part = lambda p: pl.BlockSpec((1, 1, l, HG_DK), lambda i, h, p=p: (i, p * HG_HEADS + h, 0, 0))
    return pl.pallas_call(
        functools.partial(_hgrn_kernel, seq=l),
        out_shape=jax.ShapeDtypeStruct((b, HG_HEADS, l, HG_DK), BF16),
        grid=(b, HG_HEADS),
        in_specs=[part(0), part(1), part(2), part(3), part(4),
                  pl.BlockSpec((1, 1, HG_DK), lambda i, h: (h, 0, 0)),
                  pl.BlockSpec((1, HG_DK), lambda i, h: (0, 0))],
        out_specs=pl.BlockSpec((1, 1, l, HG_DK), lambda i, h: (i, h, 0, 0)),
        scratch_shapes=[pltpu.VMEM((l, HG_DK), F32), pltpu.VMEM((l, HG_DK), F32),
                        pltpu.VMEM((HG_DK, HG_DK), F32), pltpu.VMEM((HG_DK, HG_DK), F32)],
        compiler_params=_params("parallel", "parallel"),
        name="hgrn_scan",
    )(proj, proj, proj, proj, proj, lb, gnorm)


def _band_kernel(slope_ref, q_ref, k_ref, v_ref, o_ref, l_ref, bias_ref, *, n, bq, win, dil, pairs):
    nq = n // bq
    lane = lax.broadcasted_iota(jnp.int32, (1, LANES), 1)
    head0 = lane < ATT_HD
    qsel = (jnp.where(head0, ATT_HD ** -0.5, 0.0).astype(BF16), jnp.where(head0, 0.0, ATT_HD ** -0.5).astype(BF16))
    ones0 = jnp.broadcast_to(jnp.where(head0, 1.0, 0.0).astype(BF16), (win, LANES))
    ones1 = jnp.broadcast_to(jnp.where(head0, 0.0, 1.0).astype(BF16), (win, LANES))
    delta = (lax.broadcasted_iota(jnp.int32, (bq, win), 1) - lax.broadcasted_iota(jnp.int32, (bq, win), 0))

    def make_bias(pi, offset):
        rel = jnp.abs(delta + offset)
        dist = (dil * rel).astype(F32)
        pair = pl.program_id(2) * pairs + pi
        return [jnp.where(rel <= ATT_RADIUS, -slope_ref[2 * pair + h] * dist, NEG) for h in range(2)]

    def block(pi, q0, k0, bias):
        q = q_ref[0, pi, pl.ds(q0, bq), :]
        k = k_ref[0, pi, pl.ds(k0, win), :]
        v = v_ref[0, pi, pl.ds(k0, win), :]
        ps, ms = [], []
        for h in range(2):
            s = lax.dot_general(q * qsel[h], k, _NT, preferred_element_type=F32) + bias[h]
            m = jnp.max(s, axis=-1, keepdims=True)
            ps.append(jnp.exp(s - m).astype(BF16))
            ms.append(m)
        rhs = jnp.concatenate([
            jnp.concatenate([jnp.where(head0, v, 0), ones0], axis=1),
            jnp.concatenate([jnp.where(head0, 0, v), ones1], axis=1)], axis=0)
        r = jnp.dot(jnp.concatenate(ps, axis=1), rhs, preferred_element_type=F32)
        den = r[:, LANES:]
        o_ref[0, pi, pl.ds(q0, bq), :] = (r[:, :LANES] / den).astype(o_ref.dtype)
        l_ref[0, pi, pl.ds(q0, bq), :] = jnp.where(head0, ms[0], ms[1]) + jnp.log(den)

    for pi in range(pairs):
        block(pi, 0, 0, make_bias(pi, 0))
    if nq > 2:
        for pi in range(pairs):
            b0, b1 = make_bias(pi, -ATT_RADIUS)
            bias_ref[pi, 0] = b0
            bias_ref[pi, 1] = b1
        unroll = 2 if pairs == 1 else 1

        def body(i, carry):
            for u in range(unroll):
                q0 = pl.multiple_of((i * unroll + u + 1) * bq, bq)
                k0 = pl.multiple_of(q0 - ATT_RADIUS, ATT_RADIUS)
                for pi in range(pairs):
                    block(pi, q0, k0, (bias_ref[pi, 0], bias_ref[pi, 1]))
            return carry

        lax.fori_loop(0, (nq - 2) // unroll, body, 0)
    if nq > 1:
        for pi in range(pairs):
            block(pi, n - bq, n - win, make_bias(pi, bq - win))


def band_attention(slopes, proj, group, dil):
    b, nslab, l, _ = proj.shape
    n = l // dil
    bq = min(ATT_QBLOCK, n)
    win = min(bq + 2 * ATT_RADIUS, n)
    nq = n // bq
    assert nq <= 2 or (nq - 2) % 2 == 0
    pairs = max(1, min(N_SLABS, ATT_ROWS_PER_STEP // n))
    view = proj.reshape(b, nslab, n, dil * LANES)
    part = lambda c: pl.BlockSpec((1, pairs, n, LANES), lambda bi, r, j, c=c: (bi, (3 * group + c) * (N_SLABS // pairs) + j, 0, r))
    out = pl.BlockSpec((1, pairs, n, LANES), lambda bi, r, j: (bi, j, 0, r))
    o, lse = pl.pallas_call(
        functools.partial(_band_kernel, n=n, bq=bq, win=win, dil=dil, pairs=pairs),
        out_shape=(jax.ShapeDtypeStruct((b, N_SLABS, n, dil * LANES), BF16),
                   jax.ShapeDtypeStruct((b, N_SLABS, n, dil * LANES), F32)),
        grid=(b, dil, N_SLABS // pairs),
        in_specs=[pl.BlockSpec(memory_space=pltpu.SMEM), part(0), part(1), part(2)],
        out_specs=(out, out),
        scratch_shapes=[pltpu.VMEM((pairs, 2, bq, win), F32)],
        compiler_params=_params("parallel", "parallel", "parallel"),
        name="band_attention",
    )(slopes, view, view, view)
    return o.reshape(b, N_SLABS, l, LANES), lse.reshape(b, N_SLABS, l, LANES)


def _trunk(x, mem, p):
    b, l, d = x.shape
    t = b * l
    for i in range(DEPTH):
        g = p["gains"][i]
        x = ffn_block(x.reshape(t, d), g[0], g[1], p["ffn_w_in"][i, 0], p["ffn_w_out"][i, 0]).reshape(b, l, d)
        j = i // 2
        if i % 2 == 0:
            proj = rms_matmul(x, g[2], p["hg_w_in"][j], F32, slabbed=True)
            mix = hgrn_scan(proj, p["lower_bounds"][i], p["hg_gnorm"][j])
            x = outproj_block([mix], p["hg_w_out"][j], g[3], x, tm=512)
        else:
            proj = rms_matmul(x, g[2], p["att_w_in"][j], BF16, slabbed=True)
            res = [band_attention(p["slopes"], proj, gi, dil) for gi, (_, dil) in enumerate(DIL_PATTERNS)]
            x = outproj_block([o for o, _ in res] + [lse for _, lse in res], p["att_w_out"][j], g[3], x, tm=256)
        kv = rms_matmul(mem, g[5], p["xa_w_kv"][i], BF16, slabbed=False)
        x = xattn_block(x, kv, p["xa_w_q"][i], p["xa_w_o"][i], g[4], g[6])
        x = ffn_block(x.reshape(t, d), g[7], g[8], p["ffn_w_in"][i, 1], p["ffn_w_out"][i, 1]).reshape(b, l, d)
    return x


def kernel(x_prompt, x_sample, mem_prompt, mem_sample, norm_gains, ffn_w_in, ffn_w_out, hg_w_in, hg_lb_logits, hg_gnorm, hg_w_out, att_w_in, att_w_out, xa_w_q, xa_w_kv, xa_w_o):
    sm = jax.nn.softmax(hg_lb_logits.astype(F32), axis=0)
    lower_bounds = jnp.maximum(jnp.cumsum(sm, axis=0) - sm[0], 0.0)
    p = {
        "gains": norm_gains.astype(F32).reshape(DEPTH, -1, 1, D_MODEL),
        "ffn_w_in": ffn_w_in.astype(BF16),
        "ffn_w_out": ffn_w_out.astype(BF16),
        "hg_w_in": hg_w_in.astype(BF16),
        "lower_bounds": lower_bounds.reshape(DEPTH, HG_HEADS, 1, HG_DK),
        "hg_gnorm": hg_gnorm.astype(F32).reshape(-1, 1, HG_DK),
        "hg_w_out": hg_w_out.astype(BF16),
        "att_w_in": att_w_in.astype(BF16),
        "att_w_out": att_w_out.astype(BF16),
        "xa_w_q": xa_w_q.astype(BF16),
        "xa_w_kv": xa_w_kv.astype(BF16),
        "xa_w_o": xa_w_o.astype(BF16),
        "slopes": jnp.exp2(-8.0 * jnp.arange(1, ATT_HEADS + 1, dtype=F32) / ATT_HEADS),
    }
    return (_trunk(x_prompt, mem_prompt, p), _trunk(x_sample, mem_sample, p))
```

```python
import functools

import jax
import jax.numpy as jnp
from jax import lax
from jax.experimental import pallas as pl
from jax.experimental.pallas import tpu as pltpu

F32 = jnp.float32
BF16 = jnp.bfloat16

D_MODEL = 1024
LANES = 128
N_SLABS = D_MODEL // LANES
DEPTH = 4
HG_HEADS = 8
HG_DK = 128
DIL_PATTERNS = ((128, 1), (512, 4), (2048, 16))
ATT_HEADS = 16
ATT_HD = 64
ATT_RADIUS = 64
X_HEADS = 4
X_HD = 256
D_FF = 2816
EPS = 1e-6
NEG = -1e30

HG_CHUNK = 64
HG_GROUP = 8
FF_CHUNK = 256
MM_TILE_N = 512
SUB_STRIDE = 4
ATT_QBLOCK = 128
ATT_ROWS_PER_STEP = 4096
VMEM_LIMIT = 56 * 1024 * 1024

_NT = (((1,), (1,)), ((), ()))
_TN = (((0,), (0,)), ((), ()))


def _params(*sem):
    return pltpu.CompilerParams(dimension_semantics=sem, vmem_limit_bytes=VMEM_LIMIT)


def _rms(x, g):
    return x * lax.rsqrt(jnp.mean(x * x, axis=-1, keepdims=True) + EPS) * g


def _silu(x):
    return x / (1.0 + jnp.exp(-x))


def _row_tile(t, pref):
    while t % pref:
        pref //= 2
    return pref


def _slabs(ref):
    return jnp.concatenate([ref[0, c] for c in range(ref.shape[1])], axis=-1)


def _rms_matmul_kernel(x_ref, g_ref, w_ref, o_ref, h_ref, *scratch, layout, dil):
    @pl.when(pl.program_id(2) == 0)
    def _():
        h_ref[...] = _rms(x_ref[0], g_ref[...]).astype(BF16)

    y = jnp.dot(h_ref[...], w_ref[0], preferred_element_type=F32)
    if layout == "rows":
        o_ref[0] = y.astype(o_ref.dtype)
    elif dil == 1:
        for c in range(o_ref.shape[1]):
            o_ref[0, c, 0] = y[:, c * LANES:(c + 1) * LANES].astype(o_ref.dtype)
    else:
        y_ref, *mid = scratch
        rows = y.shape[0] // dil
        d1 = SUB_STRIDE if dil > SUB_STRIDE else 1
        d2 = dil // d1
        for c in range(o_ref.shape[1]):
            y_ref[c] = y[:, c * LANES:(c + 1) * LANES]
            for q in range(d1):
                if d1 == 1:
                    src = y_ref.at[c]
                else:
                    mid[0][c, q] = y_ref[c, pl.ds(q, rows * d2, stride=d1), :]
                    src = mid[0].at[c, q]
                for p in range(d2):
                    o_ref[0, c, q + d1 * p] = src[pl.ds(p, rows, stride=d2), :].astype(o_ref.dtype)


def col_tiles(w, tn):
    *lead, d, n = w.shape
    return jnp.swapaxes(w.astype(BF16).reshape(*lead, d, n // tn, tn), -3, -2)


def rms_matmul(x, g, w_tiles, out_dtype, layout, dil=1, tm=1024):
    b, l, d = x.shape
    nt, _, tn = w_tiles.shape
    n = nt * tn
    tm = _row_tile(l, tm)
    scratch = [pltpu.VMEM((tm, d), BF16)]
    if layout == "rows":
        out_shape = (b, l, n)
        out_spec = pl.BlockSpec((1, tm, tn), lambda bi, i, j: (bi, i, j))
    else:
        out_shape = (b, n // LANES, dil, l // dil, LANES)
        out_spec = pl.BlockSpec((1, tn // LANES, dil, tm // dil, LANES), lambda bi, i, j: (bi, j, 0, i, 0))
        if dil > 1:
            scratch.append(pltpu.VMEM((tn // LANES, tm, LANES), F32))
        if dil > SUB_STRIDE:
            scratch.append(pltpu.VMEM((tn // LANES, SUB_STRIDE, tm // SUB_STRIDE, LANES), F32))
    return pl.pallas_call(
        functools.partial(_rms_matmul_kernel, layout=layout, dil=dil),
        out_shape=jax.ShapeDtypeStruct(out_shape, out_dtype),
        grid=(b, l // tm, nt),
        in_specs=[
            pl.BlockSpec((1, tm, d), lambda bi, i, j: (bi, i, 0)),
            pl.BlockSpec((1, d), lambda bi, i, j: (0, 0)),
            pl.BlockSpec((1, d, tn), lambda bi, i, j: (j, 0, 0)),
        ],
        out_specs=out_spec,
        scratch_shapes=scratch,
        compiler_params=_params("parallel", "parallel", "arbitrary"),
        name="rms_matmul",
    )(x, g, w_tiles)


def _ffn_kernel(x_ref, g0_ref, g1_ref, wg_ref, wu_ref, wo_ref, o_ref, h_ref, acc_ref):
    c = pl.program_id(1)

    @pl.when(c == 0)
    def _():
        h_ref[...] = _rms(x_ref[...], g0_ref[...]).astype(BF16)
        acc_ref[...] = jnp.zeros_like(acc_ref)

    h = h_ref[...]
    gate = jnp.dot(h, wg_ref[0], preferred_element_type=F32)
    up = jnp.dot(h, wu_ref[0], preferred_element_type=F32)
    act = (_silu(gate) * up).astype(BF16)
    acc_ref[...] += jnp.dot(act, wo_ref[...], preferred_element_type=F32)

    @pl.when(c == pl.num_programs(1) - 1)
    def _():
        o_ref[...] = x_ref[...] + 0.5 * _rms(acc_ref[...], g1_ref[...])


def ffn_block(x, g0, g1, w_in, w_out, tm=1024):
    t, d = x.shape
    tm = _row_tile(t, tm)
    nf = D_FF // FF_CHUNK
    return pl.pallas_call(
        _ffn_kernel,
        out_shape=jax.ShapeDtypeStruct((t, d), F32),
        grid=(t // tm, nf),
        in_specs=[
            pl.BlockSpec((tm, d), lambda i, c: (i, 0)),
            pl.BlockSpec((1, d), lambda i, c: (0, 0)),
            pl.BlockSpec((1, d), lambda i, c: (0, 0)),
            pl.BlockSpec((1, d, FF_CHUNK), lambda i, c: (c, 0, 0)),
            pl.BlockSpec((1, d, FF_CHUNK), lambda i, c: (c + D_FF // FF_CHUNK, 0, 0)),
            pl.BlockSpec((FF_CHUNK, d), lambda i, c: (c, 0)),
        ],
        out_specs=pl.BlockSpec((tm, d), lambda i, c: (i, 0)),
        scratch_shapes=[pltpu.VMEM((tm, d), BF16), pltpu.VMEM((tm, d), F32)],
        compiler_params=_params("parallel", "arbitrary"),
        name="ffn_block",
    )(x, g0, g1, w_in, w_in, w_out)


def _outproj_kernel(t_ref, w_ref, g_ref, x_ref, o_ref):
    y = jnp.dot(_slabs(t_ref), w_ref[...], preferred_element_type=F32)
    o_ref[0] = x_ref[0] + _rms(y, g_ref[...])


def outproj_block(mix, w, g, x, tm=512):
    b, l, d = x.shape
    tm = _row_tile(l, tm)
    row = pl.BlockSpec((1, tm, d), lambda bi, i: (bi, i, 0))
    return pl.pallas_call(
        _outproj_kernel,
        out_shape=jax.ShapeDtypeStruct((b, l, d), F32),
        grid=(b, l // tm),
        in_specs=[pl.BlockSpec((1, N_SLABS, tm, LANES), lambda bi, i: (bi, 0, i, 0)),
                  pl.BlockSpec((d, d), lambda bi, i: (0, 0)), pl.BlockSpec((1, d), lambda bi, i: (0, 0)), row],
        out_specs=row,
        compiler_params=_params("parallel", "parallel"),
        name="outproj_block",
    )(mix, w, g, x)


def _token_rows(ref, buf=None, mid=None):
    dil, sub = ref.shape[2], ref.shape[3]
    if dil == 1:
        return jnp.concatenate([ref[0, c, 0] for c in range(N_SLABS)], axis=-1).astype(F32)
    d1 = SUB_STRIDE if dil > SUB_STRIDE else 1
    d2 = dil // d1
    for c in range(N_SLABS):
        for q in range(d1):
            dst = buf.at[c] if d1 == 1 else mid.at[c, q]
            for p in range(d2):
                dst[pl.ds(p, sub, stride=d2), :] = ref[0, c, q + d1 * p].astype(F32)
            if d1 > 1:
                buf[c, pl.ds(q, sub * d2, stride=d1), :] = mid[c, q]
    return jnp.concatenate([buf[c] for c in range(N_SLABS)], axis=-1)


def _merge_outproj_kernel(o0_ref, o1_ref, o2_ref, l0_ref, l1_ref, l2_ref, w_ref, g_ref, x_ref, o_ref,
                          bl1, bl2, bo1, bo2, ml2, mo2):
    l0, l1, l2 = _token_rows(l0_ref), _token_rows(l1_ref, bl1), _token_rows(l2_ref, bl2, ml2)
    m = jnp.maximum(jnp.maximum(l0, l1), l2)
    e0, e1, e2 = jnp.exp(l0 - m), jnp.exp(l1 - m), jnp.exp(l2 - m)
    num = e0 * _token_rows(o0_ref) + e1 * _token_rows(o1_ref, bo1) + e2 * _token_rows(o2_ref, bo2, mo2)
    merged = (num / (e0 + e1 + e2)).astype(BF16)
    y = jnp.dot(merged, w_ref[...], preferred_element_type=F32)
    o_ref[0] = x_ref[0] + _rms(y, g_ref[...])


def merge_outproj_block(outs, lses, w, g, x, tm=256):
    b, l, d = x.shape
    tm = _row_tile(l, tm)
    row = pl.BlockSpec((1, tm, d), lambda bi, i: (bi, i, 0))
    planes = [pl.BlockSpec((1, N_SLABS, a.shape[2], tm // a.shape[2], LANES), lambda bi, i: (bi, 0, 0, i, 0))
              for a in outs + lses]
    return pl.pallas_call(
        _merge_outproj_kernel,
        out_shape=jax.ShapeDtypeStruct((b, l, d), F32),
        grid=(b, l // tm),
        in_specs=planes + [pl.BlockSpec((d, d), lambda bi, i: (0, 0)), pl.BlockSpec((1, d), lambda bi, i: (0, 0)), row],
        out_specs=row,
        scratch_shapes=[pltpu.VMEM((N_SLABS, tm, LANES), F32)] * 4
        + [pltpu.VMEM((N_SLABS, SUB_STRIDE, tm // SUB_STRIDE, LANES), F32)] * 2,
        compiler_params=_params("parallel", "parallel"),
        name="merge_outproj_block",
    )(*outs, *lses, w, g, x)


def _xattn_kernel(x_ref, k_ref, v_ref, wq_ref, wo_ref, gq_ref, go_ref, o_ref):
    x = x_ref[0]
    q = jnp.dot(_rms(x, gq_ref[...]).astype(BF16), wq_ref[...], preferred_element_type=F32).astype(BF16)
    k = k_ref[0]
    v = v_ref[0]
    heads = []
    for h in range(X_HEADS):
        sl = slice(h * X_HD, (h + 1) * X_HD)
        s = lax.dot_general(q[:, sl], k[:, sl], _NT, preferred_element_type=F32) * (X_HD ** -0.5)
        p = jnp.exp(s - jnp.max(s, axis=-1, keepdims=True))
        denom = jnp.sum(p, axis=-1, keepdims=True)
        heads.append(jnp.dot(p.astype(BF16), v[:, sl], preferred_element_type=F32) / denom)
    o = jnp.concatenate(heads, axis=-1).astype(BF16)
    c = jnp.dot(o, wo_ref[...], preferred_element_type=F32)
    o_ref[0] = x + _rms(c, go_ref[...])


def xattn_block(x, kv, w_q, w_o, g_q, g_o, tm=512):
    b, l, d = x.shape
    m = kv.shape[1]
    tm = _row_tile(l, tm)
    row = pl.BlockSpec((1, tm, d), lambda i, j: (i, j, 0))
    full = pl.BlockSpec((d, d), lambda i, j: (0, 0))
    gain = pl.BlockSpec((1, d), lambda i, j: (0, 0))
    return pl.pallas_call(
        _xattn_kernel,
        out_shape=jax.ShapeDtypeStruct((b, l, d), F32),
        grid=(b, l // tm),
        in_specs=[
            row,
            pl.BlockSpec((1, m, d), lambda i, j: (i, 0, 0)),
            pl.BlockSpec((1, m, d), lambda i, j: (i, 0, 1)),
            full, full, gain, gain,
        ],
        out_specs=row,
        compiler_params=_params("parallel", "parallel"),
        name="xattn_block",
    )(x, kv, kv, w_q, w_o, g_q, g_o)


def _hgrn_kernel(q_ref, zf_ref, zb_ref, v_ref, g_ref, lb_ref, gn_ref, o_ref, of_ref, ob_ref, stf_ref, stb_ref, *, seq):
    c_len = HG_CHUNK
    nc = seq // c_len
    half = c_len // 2
    lb = lb_ref[0]
    log_lb = jnp.log(lb)
    log_1m_lb = jnp.log1p(-lb)
    row = lax.broadcasted_iota(jnp.int32, (c_len, c_len), 0)
    col = lax.broadcasted_iota(jnp.int32, (c_len, c_len), 1)
    lower = col <= row
    upper = col >= row

    def gate(z):
        e = jnp.exp(-jnp.abs(z))
        log_sig = jnp.minimum(z, 0.0) - jnp.log(1.0 + e)
        t = log_1m_lb + log_sig
        logf = jnp.maximum(log_lb, t) + jnp.log(1.0 + jnp.exp(-jnp.abs(log_lb - t)))
        sig_neg = jnp.where(z >= 0.0, e, 1.0) / (1.0 + e)
        return logf, (1.0 - lb) * sig_neg

    grp = _row_tile(nc, HG_GROUP)
    ng = nc // grp
    scans = ((zf_ref, stf_ref, of_ref, lower, half - 1, c_len - 1, False),
             (zb_ref, stb_ref, ob_ref, upper, half, 0, True))

    stf_ref[...] = jnp.zeros_like(stf_ref)
    stb_ref[...] = jnp.zeros_like(stb_ref)

    def scan(i, carry):
        steps = []
        for z_ref, st_ref, out_ref, causal, ref_row, tot_row, rev in scans:
            g0 = (ng - 1 - i) if rev else i
            sls = [pl.ds(pl.multiple_of((g0 * grp + j) * c_len, c_len), c_len) for j in range(grp)]
            gates = [gate(z_ref[0, 0, sl, :]) for sl in sls]
            cums = jnp.dot(causal.astype(F32), jnp.concatenate([lf for lf, _ in gates], axis=1),
                           precision=lax.Precision.HIGHEST, preferred_element_type=F32)
            for j in (reversed(range(grp)) if rev else range(grp)):
                steps.append(dict(sl=sls[j], k=gates[j][1], cum=cums[:, j * HG_DK:(j + 1) * HG_DK], causal=causal,
                                  ref_row=ref_row, tot_row=tot_row, st_ref=st_ref, out_ref=out_ref))
        order = [steps[d * grp + j] for j in range(grp) for d in range(2)]
        for e in order:
            cum = e["cum"]
            mid = cum[e["ref_row"]:e["ref_row"] + 1, :]
            tot = cum[e["tot_row"]:e["tot_row"] + 1, :]
            q = _silu(q_ref[0, 0, e["sl"], :])
            e["v"] = v_ref[0, 0, e["sl"], :].astype(BF16)
            qm = q * jnp.exp(cum - mid)
            km = e["k"] * jnp.exp(mid - cum)
            e["qe"] = (qm * jnp.exp(mid)).astype(BF16)
            ke = (km * jnp.exp(tot - mid)).astype(BF16)
            e["decay"] = jnp.exp(tot)
            e["s"] = lax.dot_general(qm.astype(BF16), km.astype(BF16), _NT, preferred_element_type=F32)
            e["upd"] = lax.dot_general(e["v"], ke, _TN, preferred_element_type=F32)
        for e in order:
            s = jnp.where(e["causal"], e["s"], 0.0).astype(BF16)
            e["o"] = jnp.dot(s, e["v"], preferred_element_type=F32)
        states = [st_ref[...] for _, st_ref, *_ in scans]
        for idx, e in enumerate(order):
            st = states[idx % 2]
            o = e["o"] + lax.dot_general(e["qe"], st.astype(BF16), _NT, preferred_element_type=F32)
            e["out_ref"][e["sl"], :] = o
            states[idx % 2] = st * e["decay"] + e["upd"]
        for st, (_, st_ref, *_) in zip(states, scans):
            st_ref[...] = st
        return carry

    lax.fori_loop(0, ng, scan, 0)

    rows = _row_tile(seq, 256)

    def finish(i, carry):
        sl = pl.ds(pl.multiple_of(i * rows, rows), rows)
        o = of_ref[sl, :] + ob_ref[sl, :]
        o = o * lax.rsqrt(jnp.mean(o * o, axis=-1, keepdims=True) + EPS)
        o_ref[0, 0, sl, :] = (o * gn_ref[...] * _silu(g_ref[0, 0, sl, :])).astype(o_ref.dtype)
        return carry

    lax.fori_loop(0, seq // rows, finish, 0)


def hgrn_scan(proj, lb, gnorm):
    b, _, l, _ = proj.shape
    part = lambda p: pl.BlockSpec((1, 1, l, HG_DK), lambda i, h, p=p: (i, p * HG_HEADS + h, 0, 0))
    return pl.pallas_call(
        functools.partial(_hgrn_kernel, seq=l),
        out_shape=jax.ShapeDtypeStruct((b, HG_HEADS, l, HG_DK), BF16),
        grid=(b, HG_HEADS),
        in_specs=[part(0), part(1), part(2), part(3), part(4),
                  pl.BlockSpec((1, 1, HG_DK), lambda i, h: (h, 0, 0)),
                  pl.BlockSpec((1, HG_DK), lambda i, h: (0, 0))],
        out_specs=pl.BlockSpec((1, 1, l, HG_DK), lambda i, h: (i, h, 0, 0)),
        scratch_shapes=[pltpu.VMEM((l, HG_DK), F32), pltpu.VMEM((l, HG_DK), F32),
                        pltpu.VMEM((HG_DK, HG_DK), F32), pltpu.VMEM((HG_DK, HG_DK), F32)],
        compiler_params=_params("parallel", "parallel"),
        name="hgrn_scan",
    )(proj, proj, proj, proj, proj, lb, gnorm)


def _band_kernel(slope_ref, q_ref, k_ref, v_ref, o_ref, l_ref, bias_ref, *, n, bq, win, dil, pairs):
    nq = n // bq
    lane = lax.broadcasted_iota(jnp.int32, (1, LANES), 1)
    head0 = lane < ATT_HD
    qsel = (jnp.where(head0, ATT_HD ** -0.5, 0.0).astype(BF16), jnp.where(head0, 0.0, ATT_HD ** -0.5).astype(BF16))
    ones0 = jnp.broadcast_to(jnp.where(head0, 1.0, 0.0).astype(BF16), (win, LANES))
    ones1 = jnp.broadcast_to(jnp.where(head0, 0.0, 1.0).astype(BF16), (win, LANES))
    delta = (lax.broadcasted_iota(jnp.int32, (bq, win), 1) - lax.broadcasted_iota(jnp.int32, (bq, win), 0))

    def make_bias(pi, offset):
        rel = jnp.abs(delta + offset)
        dist = (dil * rel).astype(F32)
        pair = pl.program_id(2) * pairs + pi
        return [jnp.where(rel <= ATT_RADIUS, -slope_ref[2 * pair + h] * dist, NEG) for h in range(2)]

    def block(pi, q0, k0, bias):
        q = q_ref[0, pi, 0, pl.ds(q0, bq), :]
        k = k_ref[0, pi, 0, pl.ds(k0, win), :]
        v = v_ref[0, pi, 0, pl.ds(k0, win), :]
        ps, ms = [], []
        for h in range(2):
            s = lax.dot_general(q * qsel[h], k, _NT, preferred_element_type=F32) + bias[h]
            m = jnp.max(s, axis=-1, keepdims=True)
            ps.append(jnp.exp(s - m).astype(BF16))
            ms.append(m)
        rhs = jnp.concatenate([
            jnp.concatenate([jnp.where(head0, v, 0), ones0], axis=1),
            jnp.concatenate([jnp.where(head0, 0, v), ones1], axis=1)], axis=0)
        r = jnp.dot(jnp.concatenate(ps, axis=1), rhs, preferred_element_type=F32)
        den = r[:, LANES:]
        o_ref[0, pi, 0, pl.ds(q0, bq), :] = (r[:, :LANES] / den).astype(o_ref.dtype)
        l_ref[0, pi, 0, pl.ds(q0, bq), :] = jnp.where(head0, ms[0], ms[1]) + jnp.log(den)

    for pi in range(pairs):
        block(pi, 0, 0, make_bias(pi, 0))
    if nq > 2:
        for pi in range(pairs):
            b0, b1 = make_bias(pi, -ATT_RADIUS)
            bias_ref[pi, 0] = b0
            bias_ref[pi, 1] = b1
        unroll = 2 if pairs == 1 else 1

        def body(i, carry):
            for u in range(unroll):
                q0 = pl.multiple_of((i * unroll + u + 1) * bq, bq)
                k0 = pl.multiple_of(q0 - ATT_RADIUS, ATT_RADIUS)
                for pi in range(pairs):
                    block(pi, q0, k0, (bias_ref[pi, 0], bias_ref[pi, 1]))
            return carry

        lax.fori_loop(0, (nq - 2) // unroll, body, 0)
    if nq > 1:
        for pi in range(pairs):
            block(pi, n - bq, n - win, make_bias(pi, bq - win))


def band_attention(slopes, qkv):
    b, _, dil, n, _ = qkv.shape
    bq = min(ATT_QBLOCK, n)
    win = min(bq + 2 * ATT_RADIUS, n)
    nq = n // bq
    assert nq <= 2 or (nq - 2) % 2 == 0
    pairs = max(1, min(N_SLABS, ATT_ROWS_PER_STEP // n))
    part = lambda c: pl.BlockSpec((1, pairs, 1, n, LANES), lambda bi, r, j, c=c: (bi, c * (N_SLABS // pairs) + j, r, 0, 0))
    out = pl.BlockSpec((1, pairs, 1, n, LANES), lambda bi, r, j: (bi, j, r, 0, 0))
    return pl.pallas_call(
        functools.partial(_band_kernel, n=n, bq=bq, win=win, dil=dil, pairs=pairs),
        out_shape=(jax.ShapeDtypeStruct((b, N_SLABS, dil, n, LANES), BF16),
                   jax.ShapeDtypeStruct((b, N_SLABS, dil, n, LANES), F32)),
        grid=(b, dil, N_SLABS // pairs),
        in_specs=[pl.BlockSpec(memory_space=pltpu.SMEM), part(0), part(1), part(2)],
        out_specs=(out, out),
        scratch_shapes=[pltpu.VMEM((pairs, 2, bq, win), F32)],
        compiler_params=_params("parallel", "parallel", "parallel"),
        name="band_attention",
    )(slopes, qkv, qkv, qkv)


def _trunk(x, mem, p):
    b, l, d = x.shape
    t = b * l
    for i in range(DEPTH):
        g = p["gains"][i]
        x = ffn_block(x.reshape(t, d), g[0], g[1], p["ffn_w_in"][i, 0], p["ffn_w_out"][i, 0]).reshape(b, l, d)
        j = i // 2
        if i % 2 == 0:
            proj = rms_matmul(x, g[2], p["hg_w_in"][j], F32, "planes")
            mix = hgrn_scan(proj.reshape(b, -1, l, LANES), p["lower_bounds"][i], p["hg_gnorm"][j])
            x = outproj_block(mix, p["hg_w_out"][j], g[3], x)
        else:
            res = [band_attention(p["slopes"], rms_matmul(x, g[2], p["att_w_in"][j, gi], BF16, "planes", dil))
                   for gi, (_, dil) in enumerate(DIL_PATTERNS)]
            x = merge_outproj_block([o for o, _ in res], [lse for _, lse in res], p["att_w_out"][j], g[3], x)
        kv = rms_matmul(mem, g[5], p["xa_w_kv"][i], BF16, "rows")
        x = xattn_block(x, kv, p["xa_w_q"][i], p["xa_w_o"][i], g[4], g[6])
        x = ffn_block(x.reshape(t, d), g[7], g[8], p["ffn_w_in"][i, 1], p["ffn_w_out"][i, 1]).reshape(b, l, d)
    return x


def kernel(x_prompt, x_sample, mem_prompt, mem_sample, norm_gains, ffn_w_in, ffn_w_out, hg_w_in, hg_lb_logits, hg_gnorm, hg_w_out, att_w_in, att_w_out, xa_w_q, xa_w_kv, xa_w_o):
    sm = jax.nn.softmax(hg_lb_logits.astype(F32), axis=0)
    lower_bounds = jnp.maximum(jnp.cumsum(sm, axis=0) - sm[0], 0.0)
    p = {
        "gains": norm_gains.astype(F32).reshape(DEPTH, -1, 1, D_MODEL),
        "ffn_w_in": col_tiles(ffn_w_in, FF_CHUNK),
        "ffn_w_out": ffn_w_out.astype(BF16),
        "hg_w_in": col_tiles(hg_w_in, MM_TILE_N),
        "lower_bounds": lower_bounds.reshape(DEPTH, HG_HEADS, 1, HG_DK),
        "hg_gnorm": hg_gnorm.astype(F32).reshape(-1, 1, HG_DK),
        "hg_w_out": hg_w_out.astype(BF16),
        "att_w_in": col_tiles(att_w_in.reshape(-1, D_MODEL, len(DIL_PATTERNS), 3 * D_MODEL).swapaxes(1, 2), MM_TILE_N),
        "att_w_out": att_w_out.astype(BF16),
        "xa_w_q": xa_w_q.astype(BF16),
        "xa_w_kv": col_tiles(xa_w_kv, MM_TILE_N),
        "xa_w_o": xa_w_o.astype(BF16),
        "slopes": jnp.exp2(-8.0 * jnp.arange(1, ATT_HEADS + 1, dtype=F32) / ATT_HEADS),
    }
    return (_trunk(x_prompt, mem_prompt, p), _trunk(x_sample, mem_sample, p))
```

```python
import functools

import jax
import jax.numpy as jnp
from jax import lax
from jax.experimental import pallas as pl
from jax.experimental.pallas import tpu as pltpu

F32 = jnp.float32
BF16 = jnp.bfloat16

D_MODEL = 1024
LANES = 128
N_SLABS = D_MODEL // LANES
DEPTH = 4
HG_HEADS = 8
HG_DK = 128
DIL_PATTERNS = ((128, 1), (512, 4), (2048, 16))
ATT_HEADS = 16
ATT_HD = 64
ATT_RADIUS = 64
X_HEADS = 4
X_HD = 256
D_FF = 2816
EPS = 1e-6
NEG = -1e30

HG_CHUNK = 64
HG_GROUP = 8
FF_CHUNK = 256
MM_TILE_N = 512
SUB_STRIDE = 4
ATT_QBLOCK = 128
ATT_ROWS_PER_STEP = 4096
VMEM_LIMIT = 56 * 1024 * 1024

_NT = (((1,), (1,)), ((), ()))
_TN = (((0,), (0,)), ((), ()))


def _params(*sem):
    return pltpu.CompilerParams(dimension_semantics=sem, vmem_limit_bytes=VMEM_LIMIT)


def _rms(x, g):
    return x * lax.rsqrt(jnp.mean(x * x, axis=-1, keepdims=True) + EPS) * g


def _silu(x):
    return x / (1.0 + jnp.exp(-x))


def _row_tile(t, pref):
    while t % pref:
        pref //= 2
    return pref


def _slabs(ref):
    return jnp.concatenate([ref[0, c] for c in range(ref.shape[1])], axis=-1)


def _rms_matmul_kernel(x_ref, g_ref, w_ref, *refs, layout, dil, splits):
    o_refs, scratch = refs[:len(splits)], refs[len(splits):]
    h = _rms(x_ref[0], g_ref[...]).astype(BF16)
    tn = w_ref.shape[2]
    spt = tn // LANES
    for o_ref, (t0, nt) in zip(o_refs, splits):
        for j in range(nt):
            y = jnp.dot(h, w_ref[t0 + j], preferred_element_type=F32)
            if layout == "rows":
                o_ref[0, :, j * tn:(j + 1) * tn] = y.astype(o_ref.dtype)
            elif dil == 1:
                for c in range(spt):
                    o_ref[0, j * spt + c, 0] = y[:, c * LANES:(c + 1) * LANES].astype(o_ref.dtype)
            else:
                y_ref, *mid = scratch
                rows = y.shape[0] // dil
                d1 = SUB_STRIDE if dil > SUB_STRIDE else 1
                d2 = dil // d1
                for c in range(spt):
                    y_ref[c] = y[:, c * LANES:(c + 1) * LANES]
                    for q in range(d1):
                        if d1 == 1:
                            src = y_ref.at[c]
                        else:
                            mid[0][c, q] = y_ref[c, pl.ds(q, rows * d2, stride=d1), :]
                            src = mid[0].at[c, q]
                        for p in range(d2):
                            o_ref[0, j * spt + c, q + d1 * p] = src[pl.ds(p, rows, stride=d2), :].astype(o_ref.dtype)


def col_tiles(w, tn):
    *lead, d, n = w.shape
    return jnp.swapaxes(w.astype(BF16).reshape(*lead, d, n // tn, tn), -3, -2)


def rms_matmul(x, g, w_tiles, out_dtypes, layout, dil=1, tm=512):
    b, l, d = x.shape
    _, _, tn = w_tiles.shape
    tm = _row_tile(l, tm)
    out_shapes, out_specs, splits, t0 = [], [], [], 0
    for nt, dtype in out_dtypes:
        n = nt * tn
        if layout == "rows":
            out_shapes.append(jax.ShapeDtypeStruct((b, l, n), dtype))
            out_specs.append(pl.BlockSpec((1, tm, n), lambda bi, i: (bi, i, 0)))
        else:
            out_shapes.append(jax.ShapeDtypeStruct((b, n // LANES, dil, l // dil, LANES), dtype))
            out_specs.append(pl.BlockSpec((1, n // LANES, dil, tm // dil, LANES), lambda bi, i: (bi, 0, 0, i, 0)))
        splits.append((t0, nt))
        t0 += nt
    scratch = []
    if dil > 1:
        scratch.append(pltpu.VMEM((tn // LANES, tm, LANES), F32))
    if dil > SUB_STRIDE:
        scratch.append(pltpu.VMEM((tn // LANES, SUB_STRIDE, tm // SUB_STRIDE, LANES), F32))
    return pl.pallas_call(
        functools.partial(_rms_matmul_kernel, layout=layout, dil=dil, splits=tuple(splits)),
        out_shape=tuple(out_shapes),
        grid=(b, l // tm),
        in_specs=[
            pl.BlockSpec((1, tm, d), lambda bi, i: (bi, i, 0)),
            pl.BlockSpec((1, d), lambda bi, i: (0, 0)),
            pl.BlockSpec(w_tiles.shape, lambda bi, i: (0, 0, 0), pipeline_mode=pl.Buffered(1)),
        ],
        out_specs=tuple(out_specs),
        scratch_shapes=scratch,
        compiler_params=_params("parallel", "parallel"),
        name="rms_matmul",
    )(x, g, w_tiles)


def _ffn_kernel(x_ref, g0_ref, g1_ref, wi_ref, wo_ref, o_ref, acc_ref):
    nf = wo_ref.shape[0]
    h = _rms(x_ref[...], g0_ref[...]).astype(BF16)
    for c in range(nf):
        gate = jnp.dot(h, wi_ref[c], preferred_element_type=F32)
        up = jnp.dot(h, wi_ref[nf + c], preferred_element_type=F32)
        act = (_silu(gate) * up).astype(BF16)
        down = jnp.dot(act, wo_ref[c], preferred_element_type=F32)
        if c == 0:
            acc_ref[...] = down
        else:
            acc_ref[...] += down
    o_ref[...] = x_ref[...] + 0.5 * _rms(acc_ref[...], g1_ref[...])


def ffn_block(x, g0, g1, w_in, w_out, tm=512):
    t, d = x.shape
    tm = _row_tile(t, tm)
    row = pl.BlockSpec((tm, d), lambda i: (i, 0))
    gain = pl.BlockSpec((1, d), lambda i: (0, 0))
    resident = lambda w: pl.BlockSpec(w.shape, lambda i: (0, 0, 0), pipeline_mode=pl.Buffered(1))
    return pl.pallas_call(
        _ffn_kernel,
        out_shape=jax.ShapeDtypeStruct((t, d), F32),
        grid=(t // tm,),
        in_specs=[row, gain, gain, resident(w_in), resident(w_out)],
        out_specs=row,
        scratch_shapes=[pltpu.VMEM((tm, d), F32)],
        compiler_params=_params("parallel"),
        name="ffn_block",
    )(x, g0, g1, w_in, w_out)


def _outproj_kernel(t_ref, w_ref, g_ref, x_ref, o_ref):
    y = jnp.dot(_slabs(t_ref), w_ref[...], preferred_element_type=F32)
    o_ref[0] = x_ref[0] + _rms(y, g_ref[...])


def outproj_block(mix, w, g, x, tm=512):
    b, l, d = x.shape
    tm = _row_tile(l, tm)
    row = pl.BlockSpec((1, tm, d), lambda bi, i: (bi, i, 0))
    return pl.pallas_call(
        _outproj_kernel,
        out_shape=jax.ShapeDtypeStruct((b, l, d), F32),
        grid=(b, l // tm),
        in_specs=[pl.BlockSpec((1, N_SLABS, tm, LANES), lambda bi, i: (bi, 0, i, 0)),
                  pl.BlockSpec((d, d), lambda bi, i: (0, 0)), pl.BlockSpec((1, d), lambda bi, i: (0, 0)), row],
        out_specs=row,
        compiler_params=_params("parallel", "parallel"),
        name="outproj_block",
    )(mix, w, g, x)


def _token_rows(ref, buf=None, mid=None):
    dil, sub = ref.shape[2], ref.shape[3]
    if dil == 1:
        return jnp.concatenate([ref[0, c, 0] for c in range(N_SLABS)], axis=-1).astype(F32)
    d1 = SUB_STRIDE if dil > SUB_STRIDE else 1
    d2 = dil // d1
    for c in range(N_SLABS):
        for q in range(d1):
            dst = buf.at[c] if d1 == 1 else mid.at[c, q]
            for p in range(d2):
                dst[pl.ds(p, sub, stride=d2), :] = ref[0, c, q + d1 * p].astype(F32)
            if d1 > 1:
                buf[c, pl.ds(q, sub * d2, stride=d1), :] = mid[c, q]
    return jnp.concatenate([buf[c] for c in range(N_SLABS)], axis=-1)


def _merge_outproj_kernel(o0_ref, o1_ref, o2_ref, l0_ref, l1_ref, l2_ref, w_ref, g_ref, x_ref, o_ref,
                          bl1, bl2, bo1, bo2, ml2, mo2):
    l0, l1, l2 = _token_rows(l0_ref), _token_rows(l1_ref, bl1), _token_rows(l2_ref, bl2, ml2)
    m = jnp.maximum(jnp.maximum(l0, l1), l2)
    e0, e1, e2 = jnp.exp(l0 - m), jnp.exp(l1 - m), jnp.exp(l2 - m)
    num = e0 * _token_rows(o0_ref) + e1 * _token_rows(o1_ref, bo1) + e2 * _token_rows(o2_ref, bo2, mo2)
    merged = (num / (e0 + e1 + e2)).astype(BF16)
    y = jnp.dot(merged, w_ref[...], preferred_element_type=F32)
    o_ref[0] = x_ref[0] + _rms(y, g_ref[...])


def merge_outproj_block(outs, lses, w, g, x, tm=256):
    b, l, d = x.shape
    tm = _row_tile(l, tm)
    row = pl.BlockSpec((1, tm, d), lambda bi, i: (bi, i, 0))
    planes = [pl.BlockSpec((1, N_SLABS, a.shape[2], tm // a.shape[2], LANES), lambda bi, i: (bi, 0, 0, i, 0))
              for a in outs + lses]
    return pl.pallas_call(
        _merge_outproj_kernel,
        out_shape=jax.ShapeDtypeStruct((b, l, d), F32),
        grid=(b, l // tm),
        in_specs=planes + [pl.BlockSpec((d, d), lambda bi, i: (0, 0)), pl.BlockSpec((1, d), lambda bi, i: (0, 0)), row],
        out_specs=row,
        scratch_shapes=[pltpu.VMEM((N_SLABS, tm, LANES), F32)] * 4
        + [pltpu.VMEM((N_SLABS, SUB_STRIDE, tm // SUB_STRIDE, LANES), F32)] * 2,
        compiler_params=_params("parallel", "parallel"),
        name="merge_outproj_block",
    )(*outs, *lses, w, g, x)


def _xattn_kernel(x_ref, k_ref, v_ref, wq_ref, wo_ref, gq_ref, go_ref, o_ref):
    x = x_ref[0]
    q = jnp.dot(_rms(x, gq_ref[...]).astype(BF16), wq_ref[...], preferred_element_type=F32).astype(BF16)
    k = k_ref[0]
    v = v_ref[0]
    heads = []
    for h in range(X_HEADS):
        sl = slice(h * X_HD, (h + 1) * X_HD)
        s = lax.dot_general(q[:, sl], k[:, sl], _NT, preferred_element_type=F32) * (X_HD ** -0.5)
        p = jnp.exp(s - jnp.max(s, axis=-1, keepdims=True))
        denom = jnp.sum(p, axis=-1, keepdims=True)
        heads.append(jnp.dot(p.astype(BF16), v[:, sl], preferred_element_type=F32) / denom)
    o = jnp.concatenate(heads, axis=-1).astype(BF16)
    c = jnp.dot(o, wo_ref[...], preferred_element_type=F32)
    o_ref[0] = x + _rms(c, go_ref[...])


def xattn_block(x, kv, w_q, w_o, g_q, g_o, tm=512):
    b, l, d = x.shape
    m = kv.shape[1]
    tm = _row_tile(l, tm)
    row = pl.BlockSpec((1, tm, d), lambda i, j: (i, j, 0))
    full = pl.BlockSpec((d, d), lambda i, j: (0, 0))
    gain = pl.BlockSpec((1, d), lambda i, j: (0, 0))
    return pl.pallas_call(
        _xattn_kernel,
        out_shape=jax.ShapeDtypeStruct((b, l, d), F32),
        grid=(b, l // tm),
        in_specs=[
            row,
            pl.BlockSpec((1, m, d), lambda i, j: (i, 0, 0)),
            pl.BlockSpec((1, m, d), lambda i, j: (i, 0, 1)),
            full, full, gain, gain,
        ],
        out_specs=row,
        compiler_params=_params("parallel", "parallel"),
        name="xattn_block",
    )(x, kv, kv, w_q, w_o, g_q, g_o)


def _hgrn_kernel(q_ref, zf_ref, zb_ref, v_ref, g_ref, lb_ref, gn_ref, o_ref, of_ref, ob_ref, stf_ref, stb_ref, *, seq):
    c_len = HG_CHUNK
    nc = seq // c_len
    half = c_len // 2
    lb = lb_ref[0]
    log_lb = jnp.log(lb)
    log_1m_lb = jnp.log1p(-lb)
    row = lax.broadcasted_iota(jnp.int32, (c_len, c_len), 0)
    col = lax.broadcasted_iota(jnp.int32, (c_len, c_len), 1)
    lower = col <= row
    upper = col >= row

    def gate(z):
        e = jnp.exp(-jnp.abs(z))
        log_sig = jnp.minimum(z, 0.0) - jnp.log(1.0 + e)
        t = log_1m_lb + log_sig
        logf = jnp.maximum(log_lb, t) + jnp.log(1.0 + jnp.exp(-jnp.abs(log_lb - t)))
        sig_neg = jnp.where(z >= 0.0, e, 1.0) / (1.0 + e)
        return logf, (1.0 - lb) * sig_neg

    grp = _row_tile(nc, HG_GROUP)
    ng = nc // grp
    scans = ((zf_ref, stf_ref, of_ref, lower, half - 1, c_len - 1, False),
             (zb_ref, stb_ref, ob_ref, upper, half, 0, True))

    stf_ref[...] = jnp.zeros_like(stf_ref)
    stb_ref[...] = jnp.zeros_like(stb_ref)

    def scan(i, carry):
        steps = []
        for z_ref, st_ref, out_ref, causal, ref_row, tot_row, rev in scans:
            g0 = (ng - 1 - i) if rev else i
            sls = [pl.ds(pl.multiple_of((g0 * grp + j) * c_len, c_len), c_len) for j in range(grp)]
            gates = [gate(z_ref[0, 0, sl, :]) for sl in sls]
            cums = jnp.dot(causal.astype(F32), jnp.concatenate([lf for lf, _ in gates], axis=1),
                           precision=lax.Precision.HIGHEST, preferred_element_type=F32)
            for j in (reversed(range(grp)) if rev else range(grp)):
                steps.append(dict(sl=sls[j], k=gates[j][1], cum=cums[:, j * HG_DK:(j + 1) * HG_DK], causal=causal,
                                  ref_row=ref_row, tot_row=tot_row, st_ref=st_ref, out_ref=out_ref))
        order = [steps[d * grp + j] for j in range(grp) for d in range(2)]
        for e in order:
            cum = e["cum"]
            mid = cum[e["ref_row"]:e["ref_row"] + 1, :]
            tot = cum[e["tot_row"]:e["tot_row"] + 1, :]
            q = _silu(q_ref[0, 0, e["sl"], :].astype(F32))
            e["v"] = v_ref[0, 0, e["sl"], :]
            qm = q * jnp.exp(cum - mid)
            km = e["k"] * jnp.exp(mid - cum)
            e["qe"] = (qm * jnp.exp(mid)).astype(BF16)
            ke = (km * jnp.exp(tot - mid)).astype(BF16)
            e["decay"] = jnp.exp(tot)
            e["s"] = lax.dot_general(qm.astype(BF16), km.astype(BF16), _NT, preferred_element_type=F32)
            e["upd"] = lax.dot_general(e["v"], ke, _TN, preferred_element_type=F32)
        for e in order:
            s = jnp.where(e["causal"], e["s"], 0.0).astype(BF16)
            e["o"] = jnp.dot(s, e["v"], preferred_element_type=F32)
        states = [st_ref[...] for _, st_ref, *_ in scans]
        for idx, e in enumerate(order):
            st = states[idx % 2]
            o = e["o"] + lax.dot_general(e["qe"], st.astype(BF16), _NT, preferred_element_type=F32)
            e["out_ref"][e["sl"], :] = o
            states[idx % 2] = st * e["decay"] + e["upd"]
        for st, (_, st_ref, *_) in zip(states, scans):
            st_ref[...] = st
        return carry

    lax.fori_loop(0, ng, scan, 0)

    rows = _row_tile(seq, 256)

    def finish(i, carry):
        sl = pl.ds(pl.multiple_of(i * rows, rows), rows)
        o = of_ref[sl, :] + ob_ref[sl, :]
        o = o * lax.rsqrt(jnp.mean(o * o, axis=-1, keepdims=True) + EPS)
        o_ref[0, 0, sl, :] = (o * gn_ref[...] * _silu(g_ref[0, 0, sl, :].astype(F32))).astype(o_ref.dtype)
        return carry

    lax.fori_loop(0, seq // rows, finish, 0)


def hgrn_scan(qvg, gates, lb, gnorm):
    b, _, l, _ = qvg.shape
    part = lambda p: pl.BlockSpec((1, 1, l, HG_DK), lambda i, h, p=p: (i, p * HG_HEADS + h, 0, 0))
    return pl.pallas_call(
        functools.partial(_hgrn_kernel, seq=l),
        out_shape=jax.ShapeDtypeStruct((b, HG_HEADS, l, HG_DK), BF16),
        grid=(b, HG_HEADS),
        in_specs=[part(0), part(0), part(1), part(1), part(2),
                  pl.BlockSpec((1, 1, HG_DK), lambda i, h: (h, 0, 0)),
                  pl.BlockSpec((1, HG_DK), lambda i, h: (0, 0))],
        out_specs=pl.BlockSpec((1, 1, l, HG_DK), lambda i, h: (i, h, 0, 0)),
        scratch_shapes=[pltpu.VMEM((l, HG_DK), F32), pltpu.VMEM((l, HG_DK), F32),
                        pltpu.VMEM((HG_DK, HG_DK), F32), pltpu.VMEM((HG_DK, HG_DK), F32)],
        compiler_params=_params("parallel", "parallel"),
        name="hgrn_scan",
    )(qvg, gates, gates, qvg, qvg, lb, gnorm)


def _band_kernel(slope_ref, q_ref, k_ref, v_ref, o_ref, l_ref, bias_ref, *, n, bq, win, dil, pairs):
    nq = n // bq
    lane = lax.broadcasted_iota(jnp.int32, (1, LANES), 1)
    head0 = lane < ATT_HD
    qsel = (jnp.where(head0, ATT_HD ** -0.5, 0.0).astype(BF16), jnp.where(head0, 0.0, ATT_HD ** -0.5).astype(BF16))
    ones0 = jnp.broadcast_to(jnp.where(head0, 1.0, 0.0).astype(BF16), (win, LANES))
    ones1 = jnp.broadcast_to(jnp.where(head0, 0.0, 1.0).astype(BF16), (win, LANES))
    delta = (lax.broadcasted_iota(jnp.int32, (bq, win), 1) - lax.broadcasted_iota(jnp.int32, (bq, win), 0))

    def make_bias(pi, offset):
        rel = jnp.abs(delta + offset)
        dist = (dil * rel).astype(F32)
        pair = pl.program_id(2) * pairs + pi
        return [jnp.where(rel <= ATT_RADIUS, -slope_ref[2 * pair + h] * dist, NEG) for h in range(2)]

    def block(pi, q0, k0, bias):
        q = q_ref[0, pi, 0, pl.ds(q0, bq), :]
        k = k_ref[0, pi, 0, pl.ds(k0, win), :]
        v = v_ref[0, pi, 0, pl.ds(k0, win), :]
        ps, ms = [], []
        for h in range(2):
            s = lax.dot_general(q * qsel[h], k, _NT, preferred_element_type=F32) + bias[h]
            m = jnp.max(s, axis=-1, keepdims=True)
            ps.append(jnp.exp(s - m).astype(BF16))
            ms.append(m)
        rhs = jnp.concatenate([
            jnp.concatenate([jnp.where(head0, v, 0), ones0], axis=1),
            jnp.concatenate([jnp.where(head0, 0, v), ones1], axis=1)], axis=0)
        r = jnp.dot(jnp.concatenate(ps, axis=1), rhs, preferred_element_type=F32)
        den = r[:, LANES:]
        o_ref[0, pi, 0, pl.ds(q0, bq), :] = (r[:, :LANES] / den).astype(o_ref.dtype)
        l_ref[0, pi, 0, pl.ds(q0, bq), :] = jnp.where(head0, ms[0], ms[1]) + jnp.log(den)

    for pi in range(pairs):
        block(pi, 0, 0, make_bias(pi, 0))
    if nq > 2:
        for pi in range(pairs):
            b0, b1 = make_bias(pi, -ATT_RADIUS)
            bias_ref[pi, 0] = b0
            bias_ref[pi, 1] = b1
        unroll = 2 if pairs == 1 else 1

        def body(i, carry):
            for u in range(unroll):
                q0 = pl.multiple_of((i * unroll + u + 1) * bq, bq)
                k0 = pl.multiple_of(q0 - ATT_RADIUS, ATT_RADIUS)
                for pi in range(pairs):
                    block(pi, q0, k0, (bias_ref[pi, 0], bias_ref[pi, 1]))
            return carry

        lax.fori_loop(0, (nq - 2) // unroll, body, 0)
    if nq > 1:
        for pi in range(pairs):
            block(pi, n - bq, n - win, make_bias(pi, bq - win))


def band_attention(slopes, qkv):
    b, _, dil, n, _ = qkv.shape
    bq = min(ATT_QBLOCK, n)
    win = min(bq + 2 * ATT_RADIUS, n)
    nq = n // bq
    assert nq <= 2 or (nq - 2) % 2 == 0
    pairs = max(1, min(N_SLABS, ATT_ROWS_PER_STEP // n))
    part = lambda c: pl.BlockSpec((1, pairs, 1, n, LANES), lambda bi, r, j, c=c: (bi, c * (N_SLABS // pairs) + j, r, 0, 0))
    out = pl.BlockSpec((1, pairs, 1, n, LANES), lambda bi, r, j: (bi, j, r, 0, 0))
    return pl.pallas_call(
        functools.partial(_band_kernel, n=n, bq=bq, win=win, dil=dil, pairs=pairs),
        out_shape=(jax.ShapeDtypeStruct((b, N_SLABS, dil, n, LANES), BF16),
                   jax.ShapeDtypeStruct((b, N_SLABS, dil, n, LANES), F32)),
        grid=(b, dil, N_SLABS // pairs),
        in_specs=[pl.BlockSpec(memory_space=pltpu.SMEM), part(0), part(1), part(2)],
        out_specs=(out, out),
        scratch_shapes=[pltpu.VMEM((pairs, 2, bq, win), F32)],
        compiler_params=_params("parallel", "parallel", "parallel"),
        name="band_attention",
    )(slopes, qkv, qkv, qkv)


def _trunk(x, mem, p):
    b, l, d = x.shape
    t = b * l
    for i in range(DEPTH):
        g = p["gains"][i]
        x = ffn_block(x.reshape(t, d), g[0], g[1], p["ffn_w_in"][i, 0], p["ffn_w_out"][i, 0]).reshape(b, l, d)
        j = i // 2
        if i % 2 == 0:
            nt = D_MODEL // MM_TILE_N
            qvg, gates = rms_matmul(x, g[2], p["hg_w_in"][j], [(3 * nt, BF16), (2 * nt, F32)], "planes")
            mix = hgrn_scan(qvg.reshape(b, -1, l, LANES), gates.reshape(b, -1, l, LANES),
                            p["lower_bounds"][i], p["hg_gnorm"][j])
            x = outproj_block(mix, p["hg_w_out"][j], g[3], x)
        else:
            nt = 3 * D_MODEL // MM_TILE_N
            res = [band_attention(p["slopes"], *rms_matmul(x, g[2], p["att_w_in"][j, gi], [(nt, BF16)], "planes", dil))
                   for gi, (_, dil) in enumerate(DIL_PATTERNS)]
            x = merge_outproj_block([o for o, _ in res], [lse for _, lse in res], p["att_w_out"][j], g[3], x)
        kv, = rms_matmul(mem, g[5], p["xa_w_kv"][i], [(2 * D_MODEL // MM_TILE_N, BF16)], "rows")
        x = xattn_block(x, kv, p["xa_w_q"][i], p["xa_w_o"][i], g[4], g[6])
        x = ffn_block(x.reshape(t, d), g[7], g[8], p["ffn_w_in"][i, 1], p["ffn_w_out"][i, 1]).reshape(b, l, d)
    return x


def kernel(x_prompt, x_sample, mem_prompt, mem_sample, norm_gains, ffn_w_in, ffn_w_out, hg_w_in, hg_lb_logits, hg_gnorm, hg_w_out, att_w_in, att_w_out, xa_w_q, xa_w_kv, xa_w_o):
    sm = jax.nn.softmax(hg_lb_logits.astype(F32), axis=0)
    lower_bounds = jnp.maximum(jnp.cumsum(sm, axis=0) - sm[0], 0.0)
    p = {
        "gains": norm_gains.astype(F32).reshape(DEPTH, -1, 1, D_MODEL),
        "ffn_w_in": col_tiles(ffn_w_in, FF_CHUNK),
        "ffn_w_out": ffn_w_out.astype(BF16).reshape(DEPTH, 2, D_FF // FF_CHUNK, FF_CHUNK, D_MODEL),
        "hg_w_in": col_tiles(jnp.concatenate([hg_w_in[..., s * D_MODEL:(s + 1) * D_MODEL] for s in (0, 3, 4, 1, 2)],
                                             axis=-1), MM_TILE_N),
        "lower_bounds": lower_bounds.reshape(DEPTH, HG_HEADS, 1, HG_DK),
        "hg_gnorm": hg_gnorm.astype(F32).reshape(-1, 1, HG_DK),
        "hg_w_out": hg_w_out.astype(BF16),
        "att_w_in": col_tiles(att_w_in.reshape(-1, D_MODEL, len(DIL_PATTERNS), 3 * D_MODEL).swapaxes(1, 2), MM_TILE_N),
        "att_w_out": att_w_out.astype(BF16),
        "xa_w_q": xa_w_q.astype(BF16),
        "xa_w_kv": col_tiles(xa_w_kv, MM_TILE_N),
        "xa_w_o": xa_w_o.astype(BF16),
        "slopes": jnp.exp2(-8.0 * jnp.arange(1, ATT_HEADS + 1, dtype=F32) / ATT_HEADS),
    }
    return (_trunk(x_prompt, mem_prompt, p), _trunk(x_sample, mem_sample, p))
```

```python
import functools

import jax
import jax.numpy as jnp
from jax import lax
from jax.experimental import pallas as pl
from jax.experimental.pallas import tpu as pltpu

F32 = jnp.float32
BF16 = jnp.bfloat16

D_MODEL = 1024
LANES = 128
N_SLABS = D_MODEL // LANES
DEPTH = 4
HG_HEADS = 8
HG_DK = 128
DIL_PATTERNS = ((128, 1), (512, 4), (2048, 16))
ATT_HEADS = 16
ATT_HD = 64
ATT_RADIUS = 64
X_HEADS = 4
X_HD = 256
D_FF = 2816
EPS = 1e-6
NEG = -1e30

HG_CHUNK = 64
HG_GROUP = 8
FF_CHUNK = 256
MM_TILE_N = 512
SUB_STRIDE = 4
ATT_QBLOCK = 128
ATT_INFLIGHT = 8
ATT_ROWS_PER_STEP = 4096
VMEM_LIMIT = 56 * 1024 * 1024

_NT = (((1,), (1,)), ((), ()))
_TN = (((0,), (0,)), ((), ()))


def _params(*sem):
    return pltpu.CompilerParams(dimension_semantics=sem, vmem_limit_bytes=VMEM_LIMIT)


def _rms(x, g):
    return x * lax.rsqrt(jnp.mean(x * x, axis=-1, keepdims=True) + EPS) * g


def _silu(x):
    return x / (1.0 + jnp.exp(-x))


def _row_tile(t, pref):
    while t % pref:
        pref //= 2
    return pref


def _slabs(ref):
    return jnp.concatenate([ref[0, c] for c in range(ref.shape[1])], axis=-1)


def _rms_matmul_kernel(x_ref, g_ref, w_ref, *refs, layout, dil, splits):
    o_refs, scratch = refs[:len(splits)], refs[len(splits):]
    h = _rms(x_ref[0], g_ref[...]).astype(BF16)
    tn = w_ref.shape[2]
    spt = tn // LANES
    for o_ref, (t0, nt) in zip(o_refs, splits):
        for j in range(nt):
            y = jnp.dot(h, w_ref[t0 + j], preferred_element_type=F32)
            if layout == "rows":
                o_ref[0, :, j * tn:(j + 1) * tn] = y.astype(o_ref.dtype)
            elif dil == 1:
                for c in range(spt):
                    o_ref[0, j * spt + c, 0] = y[:, c * LANES:(c + 1) * LANES].astype(o_ref.dtype)
            else:
                y_ref, *mid = scratch
                rows = y.shape[0] // dil
                d1 = SUB_STRIDE if dil > SUB_STRIDE else 1
                d2 = dil // d1
                for c in range(spt):
                    y_ref[c] = y[:, c * LANES:(c + 1) * LANES]
                    for q in range(d1):
                        if d1 == 1:
                            src = y_ref.at[c]
                        else:
                            mid[0][c, q] = y_ref[c, pl.ds(q, rows * d2, stride=d1), :]
                            src = mid[0].at[c, q]
                        for p in range(d2):
                            o_ref[0, j * spt + c, q + d1 * p] = src[pl.ds(p, rows, stride=d2), :].astype(o_ref.dtype)


def col_tiles(w, tn):
    *lead, d, n = w.shape
    return jnp.swapaxes(w.astype(BF16).reshape(*lead, d, n // tn, tn), -3, -2)


def rms_matmul(x, g, w_tiles, out_dtypes, layout, dil=1, tm=512):
    b, l, d = x.shape
    _, _, tn = w_tiles.shape
    tm = _row_tile(l, tm)
    out_shapes, out_specs, splits, t0 = [], [], [], 0
    for nt, dtype in out_dtypes:
        n = nt * tn
        if layout == "rows":
            out_shapes.append(jax.ShapeDtypeStruct((b, l, n), dtype))
            out_specs.append(pl.BlockSpec((1, tm, n), lambda bi, i: (bi, i, 0)))
        else:
            out_shapes.append(jax.ShapeDtypeStruct((b, n // LANES, dil, l // dil, LANES), dtype))
            out_specs.append(pl.BlockSpec((1, n // LANES, dil, tm // dil, LANES), lambda bi, i: (bi, 0, 0, i, 0)))
        splits.append((t0, nt))
        t0 += nt
    scratch = []
    if dil > 1:
        scratch.append(pltpu.VMEM((tn // LANES, tm, LANES), F32))
    if dil > SUB_STRIDE:
        scratch.append(pltpu.VMEM((tn // LANES, SUB_STRIDE, tm // SUB_STRIDE, LANES), F32))
    return pl.pallas_call(
        functools.partial(_rms_matmul_kernel, layout=layout, dil=dil, splits=tuple(splits)),
        out_shape=tuple(out_shapes),
        grid=(b, l // tm),
        in_specs=[
            pl.BlockSpec((1, tm, d), lambda bi, i: (bi, i, 0)),
            pl.BlockSpec((1, d), lambda bi, i: (0, 0)),
            pl.BlockSpec(w_tiles.shape, lambda bi, i: (0, 0, 0), pipeline_mode=pl.Buffered(1)),
        ],
        out_specs=tuple(out_specs),
        scratch_shapes=scratch,
        compiler_params=_params("parallel", "parallel"),
        name="rms_matmul",
    )(x, g, w_tiles)


def _ffn_kernel(x_ref, g0_ref, g1_ref, wi_ref, wo_ref, o_ref, acc_ref):
    nf = wo_ref.shape[0]
    h = _rms(x_ref[...], g0_ref[...]).astype(BF16)
    for c in range(nf):
        gate = jnp.dot(h, wi_ref[c], preferred_element_type=F32)
        up = jnp.dot(h, wi_ref[nf + c], preferred_element_type=F32)
        act = (_silu(gate) * up).astype(BF16)
        down = jnp.dot(act, wo_ref[c], preferred_element_type=F32)
        if c == 0:
            acc_ref[...] = down
        else:
            acc_ref[...] += down
    o_ref[...] = x_ref[...] + 0.5 * _rms(acc_ref[...], g1_ref[...])


def ffn_block(x, g0, g1, w_in, w_out, tm=512):
    t, d = x.shape
    tm = _row_tile(t, tm)
    row = pl.BlockSpec((tm, d), lambda i: (i, 0))
    gain = pl.BlockSpec((1, d), lambda i: (0, 0))
    resident = lambda w: pl.BlockSpec(w.shape, lambda i: (0, 0, 0), pipeline_mode=pl.Buffered(1))
    return pl.pallas_call(
        _ffn_kernel,
        out_shape=jax.ShapeDtypeStruct((t, d), F32),
        grid=(t // tm,),
        in_specs=[row, gain, gain, resident(w_in), resident(w_out)],
        out_specs=row,
        scratch_shapes=[pltpu.VMEM((tm, d), F32)],
        compiler_params=_params("parallel"),
        name="ffn_block",
    )(x, g0, g1, w_in, w_out)


def _outproj_kernel(t_ref, w_ref, g_ref, x_ref, o_ref):
    y = jnp.dot(_slabs(t_ref), w_ref[...], preferred_element_type=F32)
    o_ref[0] = x_ref[0] + _rms(y, g_ref[...])


def outproj_block(mix, w, g, x, tm=512):
    b, l, d = x.shape
    tm = _row_tile(l, tm)
    row = pl.BlockSpec((1, tm, d), lambda bi, i: (bi, i, 0))
    return pl.pallas_call(
        _outproj_kernel,
        out_shape=jax.ShapeDtypeStruct((b, l, d), F32),
        grid=(b, l // tm),
        in_specs=[pl.BlockSpec((1, N_SLABS, tm, LANES), lambda bi, i: (bi, 0, i, 0)),
                  pl.BlockSpec((d, d), lambda bi, i: (0, 0)), pl.BlockSpec((1, d), lambda bi, i: (0, 0)), row],
        out_specs=row,
        compiler_params=_params("parallel", "parallel"),
        name="outproj_block",
    )(mix, w, g, x)


def _token_rows(ref, buf=None, mid=None):
    dil, sub = ref.shape[2], ref.shape[3]
    if dil == 1:
        return jnp.concatenate([ref[0, c, 0] for c in range(N_SLABS)], axis=-1).astype(F32)
    d1 = SUB_STRIDE if dil > SUB_STRIDE else 1
    d2 = dil // d1
    for c in range(N_SLABS):
        for q in range(d1):
            dst = buf.at[c] if d1 == 1 else mid.at[c, q]
            for p in range(d2):
                dst[pl.ds(p, sub, stride=d2), :] = ref[0, c, q + d1 * p].astype(F32)
            if d1 > 1:
                buf[c, pl.ds(q, sub * d2, stride=d1), :] = mid[c, q]
    return jnp.concatenate([buf[c] for c in range(N_SLABS)], axis=-1)


def _merge_outproj_kernel(o0_ref, o1_ref, o2_ref, l0_ref, l1_ref, l2_ref, w_ref, g_ref, x_ref, o_ref,
                          bl1, bl2, bo1, bo2, ml2, mo2):
    l0, l1, l2 = _token_rows(l0_ref), _token_rows(l1_ref, bl1), _token_rows(l2_ref, bl2, ml2)
    m = jnp.maximum(jnp.maximum(l0, l1), l2)
    e0, e1, e2 = jnp.exp(l0 - m), jnp.exp(l1 - m), jnp.exp(l2 - m)
    num = e0 * _token_rows(o0_ref) + e1 * _token_rows(o1_ref, bo1) + e2 * _token_rows(o2_ref, bo2, mo2)
    merged = (num / (e0 + e1 + e2)).astype(BF16)
    y = jnp.dot(merged, w_ref[...], preferred_element_type=F32)
    o_ref[0] = x_ref[0] + _rms(y, g_ref[...])


def merge_outproj_block(outs, lses, w, g, x, tm=256):
    b, l, d = x.shape
    tm = _row_tile(l, tm)
    row = pl.BlockSpec((1, tm, d), lambda bi, i: (bi, i, 0))
    planes = [pl.BlockSpec((1, N_SLABS, a.shape[2], tm // a.shape[2], LANES), lambda bi, i: (bi, 0, 0, i, 0))
              for a in outs + lses]
    return pl.pallas_call(
        _merge_outproj_kernel,
        out_shape=jax.ShapeDtypeStruct((b, l, d), F32),
        grid=(b, l // tm),
        in_specs=planes + [pl.BlockSpec((d, d), lambda bi, i: (0, 0)), pl.BlockSpec((1, d), lambda bi, i: (0, 0)), row],
        out_specs=row,
        scratch_shapes=[pltpu.VMEM((N_SLABS, tm, LANES), F32)] * 4
        + [pltpu.VMEM((N_SLABS, SUB_STRIDE, tm // SUB_STRIDE, LANES), F32)] * 2,
        compiler_params=_params("parallel", "parallel"),
        name="merge_outproj_block",
    )(*outs, *lses, w, g, x)


def _xattn_kernel(x_ref, k_ref, v_ref, wq_ref, wo_ref, gq_ref, go_ref, o_ref):
    x = x_ref[0]
    q = jnp.dot(_rms(x, gq_ref[...]).astype(BF16), wq_ref[...], preferred_element_type=F32).astype(BF16)
    k = k_ref[0]
    v = v_ref[0]
    heads = []
    for h in range(X_HEADS):
        sl = slice(h * X_HD, (h + 1) * X_HD)
        s = lax.dot_general(q[:, sl], k[:, sl], _NT, preferred_element_type=F32) * (X_HD ** -0.5)
        p = jnp.exp(s - jnp.max(s, axis=-1, keepdims=True))
        denom = jnp.sum(p, axis=-1, keepdims=True)
        heads.append(jnp.dot(p.astype(BF16), v[:, sl], preferred_element_type=F32) / denom)
    o = jnp.concatenate(heads, axis=-1).astype(BF16)
    c = jnp.dot(o, wo_ref[...], preferred_element_type=F32)
    o_ref[0] = x + _rms(c, go_ref[...])


def xattn_block(x, kv, w_q, w_o, g_q, g_o, tm=512):
    b, l, d = x.shape
    m = kv.shape[1]
    tm = _row_tile(l, tm)
    row = pl.BlockSpec((1, tm, d), lambda i, j: (i, j, 0))
    full = pl.BlockSpec((d, d), lambda i, j: (0, 0))
    gain = pl.BlockSpec((1, d), lambda i, j: (0, 0))
    return pl.pallas_call(
        _xattn_kernel,
        out_shape=jax.ShapeDtypeStruct((b, l, d), F32),
        grid=(b, l // tm),
        in_specs=[
            row,
            pl.BlockSpec((1, m, d), lambda i, j: (i, 0, 0)),
            pl.BlockSpec((1, m, d), lambda i, j: (i, 0, 1)),
            full, full, gain, gain,
        ],
        out_specs=row,
        compiler_params=_params("parallel", "parallel"),
        name="xattn_block",
    )(x, kv, kv, w_q, w_o, g_q, g_o)


def _hgrn_kernel(q_ref, zf_ref, zb_ref, v_ref, g_ref, lb_ref, gn_ref, o_ref, of_ref, ob_ref, stf_ref, stb_ref, *, seq):
    c_len = HG_CHUNK
    nc = seq // c_len
    half = c_len // 2
    lb = lb_ref[0]
    log_lb = jnp.log(lb)
    log_1m_lb = jnp.log1p(-lb)
    row = lax.broadcasted_iota(jnp.int32, (c_len, c_len), 0)
    col = lax.broadcasted_iota(jnp.int32, (c_len, c_len), 1)
    same_half = (row < half) == (col < half)
    lower = (col <= row) & same_half
    upper = (col >= row) & same_half
    lower_sum = (col <= row).astype(F32)
    upper_sum = (col >= row).astype(F32)

    def halves(first, second):
        return jnp.concatenate([jnp.broadcast_to(first, (half, HG_DK)), jnp.broadcast_to(second, (half, HG_DK))], axis=0)

    def gate(z):
        e = jnp.exp(-jnp.abs(z))
        log_sig = jnp.minimum(z, 0.0) - jnp.log(1.0 + e)
        t = log_1m_lb + log_sig
        logf = jnp.maximum(log_lb, t) + jnp.log(1.0 + jnp.exp(-jnp.abs(log_lb - t)))
        sig_neg = jnp.where(z >= 0.0, e, 1.0) / (1.0 + e)
        return logf, (1.0 - lb) * sig_neg

    grp = _row_tile(nc, HG_GROUP)
    ng = nc // grp
    quarter = half // 2
    scans = ((zf_ref, stf_ref, of_ref, lower_sum, lower, (quarter - 1, half - 1, half + quarter - 1, c_len - 1), False),
             (zb_ref, stb_ref, ob_ref, upper_sum, upper, (quarter, half, half + quarter, 0), True))

    stf_ref[...] = jnp.zeros_like(stf_ref)
    stb_ref[...] = jnp.zeros_like(stb_ref)

    def scan(i, carry):
        steps = []
        for z_ref, st_ref, out_ref, sum_mat, causal, ref_rows, rev in scans:
            g0 = (ng - 1 - i) if rev else i
            sls = [pl.ds(pl.multiple_of((g0 * grp + j) * c_len, c_len), c_len) for j in range(grp)]
            gates = [gate(z_ref[0, 0, sl, :]) for sl in sls]
            cums = jnp.dot(sum_mat, jnp.concatenate([lf for lf, _ in gates], axis=1),
                           precision=lax.Precision.HIGHEST, preferred_element_type=F32)
            for j in (reversed(range(grp)) if rev else range(grp)):
                steps.append(dict(sl=sls[j], k=gates[j][1], cum=cums[:, j * HG_DK:(j + 1) * HG_DK], causal=causal,
                                  ref_rows=ref_rows, rev=rev, st_ref=st_ref, out_ref=out_ref))
        order = [steps[d * grp + j] for j in range(grp) for d in range(2)]
        zeros = jnp.zeros((half, HG_DK), F32)
        for e in order:
            cum = e["cum"]
            m_a, m_mid, m_b, tot = (cum[r:r + 1, :] for r in e["ref_rows"])
            q = _silu(q_ref[0, 0, e["sl"], :].astype(F32))
            e["v"] = v_ref[0, 0, e["sl"], :]
            ref = halves(m_a, m_b)
            qd = q * jnp.exp(cum - ref)
            kd = e["k"] * jnp.exp(ref - cum)
            e["qe"] = (qd * halves(jnp.exp(m_a), jnp.exp(m_b))).astype(BF16)
            ke = (kd * halves(jnp.exp(tot - m_a), jnp.exp(tot - m_b))).astype(BF16)
            e["decay"] = jnp.exp(tot)
            if e["rev"]:
                qx = jnp.concatenate([qd[:half] * jnp.exp(m_a - m_mid), zeros], axis=0)
                kx = jnp.concatenate([zeros, kd[half:] * jnp.exp(m_mid - m_b)], axis=0)
            else:
                qx = jnp.concatenate([zeros, qd[half:] * jnp.exp(m_b - m_mid)], axis=0)
                kx = jnp.concatenate([kd[:half] * jnp.exp(m_mid - m_a), zeros], axis=0)
            e["s_in"] = lax.dot_general(qd.astype(BF16), kd.astype(BF16), _NT, preferred_element_type=F32)
            e["s_x"] = lax.dot_general(qx.astype(BF16), kx.astype(BF16), _NT, preferred_element_type=F32)
            e["upd"] = lax.dot_general(e["v"], ke, _TN, preferred_element_type=F32)
        for e in order:
            s = (jnp.where(e["causal"], e["s_in"], 0.0) + e["s_x"]).astype(BF16)
            e["o"] = jnp.dot(s, e["v"], preferred_element_type=F32)
        states = [st_ref[...] for _, st_ref, *_ in scans]
        for idx, e in enumerate(order):
            st = states[idx % 2]
            o = e["o"] + lax.dot_general(e["qe"], st.astype(BF16), _NT, preferred_element_type=F32)
            e["out_ref"][e["sl"], :] = o
            states[idx % 2] = st * e["decay"] + e["upd"]
        for st, (_, st_ref, *_) in zip(states, scans):
            st_ref[...] = st
        return carry

    lax.fori_loop(0, ng, scan, 0)

    rows = _row_tile(seq, 256)

    def finish(i, carry):
        sl = pl.ds(pl.multiple_of(i * rows, rows), rows)
        o = of_ref[sl, :] + ob_ref[sl, :]
        o = o * lax.rsqrt(jnp.mean(o * o, axis=-1, keepdims=True) + EPS)
        o_ref[0, 0, sl, :] = (o * gn_ref[...] * _silu(g_ref[0, 0, sl, :].astype(F32))).astype(o_ref.dtype)
        return carry

    lax.fori_loop(0, seq // rows, finish, 0)


def hgrn_scan(qvg, gates, lb, gnorm):
    b, _, l, _ = qvg.shape
    part = lambda p: pl.BlockSpec((1, 1, l, HG_DK), lambda i, h, p=p: (i, p * HG_HEADS + h, 0, 0))
    return pl.pallas_call(
        functools.partial(_hgrn_kernel, seq=l),
        out_shape=jax.ShapeDtypeStruct((b, HG_HEADS, l, HG_DK), BF16),
        grid=(b, HG_HEADS),
        in_specs=[part(0), part(0), part(1), part(1), part(2),
                  pl.BlockSpec((1, 1, HG_DK), lambda i, h: (h, 0, 0)),
                  pl.BlockSpec((1, HG_DK), lambda i, h: (0, 0))],
        out_specs=pl.BlockSpec((1, 1, l, HG_DK), lambda i, h: (i, h, 0, 0)),
        scratch_shapes=[pltpu.VMEM((l, HG_DK), F32), pltpu.VMEM((l, HG_DK), F32),
                        pltpu.VMEM((HG_DK, HG_DK), F32), pltpu.VMEM((HG_DK, HG_DK), F32)],
        compiler_params=_params("parallel", "parallel"),
        name="hgrn_scan",
    )(qvg, gates, gates, qvg, qvg, lb, gnorm)


def _band_kernel(slope_ref, q_ref, k_ref, v_ref, o_ref, l_ref, bias_ref, *, n, bq, win, dil, pairs):
    nq = n // bq
    lane = lax.broadcasted_iota(jnp.int32, (1, LANES), 1)
    head0 = lane < ATT_HD
    qsel = (jnp.where(head0, ATT_HD ** -0.5, 0.0).astype(BF16), jnp.where(head0, 0.0, ATT_HD ** -0.5).astype(BF16))
    ones0 = jnp.broadcast_to(jnp.where(head0, 1.0, 0.0).astype(BF16), (win, LANES))
    ones1 = jnp.broadcast_to(jnp.where(head0, 0.0, 1.0).astype(BF16), (win, LANES))
    delta = (lax.broadcasted_iota(jnp.int32, (bq, win), 1) - lax.broadcasted_iota(jnp.int32, (bq, win), 0))

    def make_bias(pi, offset):
        rel = jnp.abs(delta + offset)
        dist = (dil * rel).astype(F32)
        pair = pl.program_id(2) * pairs + pi
        return [jnp.where(rel <= ATT_RADIUS, -slope_ref[2 * pair + h] * dist, NEG) for h in range(2)]

    def block(pi, q0, k0, bias):
        q = q_ref[0, pi, 0, pl.ds(q0, bq), :]
        k = k_ref[0, pi, 0, pl.ds(k0, win), :]
        v = v_ref[0, pi, 0, pl.ds(k0, win), :]
        ps, ms = [], []
        for h in range(2):
            s = lax.dot_general(q * qsel[h], k, _NT, preferred_element_type=F32) + bias[h]
            m = jnp.max(s, axis=-1, keepdims=True)
            ps.append(jnp.exp(s - m).astype(BF16))
            ms.append(m)
        rhs = jnp.concatenate([
            jnp.concatenate([jnp.where(head0, v, 0), ones0], axis=1),
            jnp.concatenate([jnp.where(head0, 0, v), ones1], axis=1)], axis=0)
        r = jnp.dot(jnp.concatenate(ps, axis=1), rhs, preferred_element_type=F32)
        den = r[:, LANES:]
        o_ref[0, pi, 0, pl.ds(q0, bq), :] = (r[:, :LANES] / den).astype(o_ref.dtype)
        l_ref[0, pi, 0, pl.ds(q0, bq), :] = jnp.where(head0, ms[0], ms[1]) + jnp.log(den)

    for pi in range(pairs):
        block(pi, 0, 0, make_bias(pi, 0))
    if nq > 2:
        for pi in range(pairs):
            b0, b1 = make_bias(pi, -ATT_RADIUS)
            bias_ref[pi, 0] = b0
            bias_ref[pi, 1] = b1
        unroll = max(1, ATT_INFLIGHT // pairs)
        trips, rest = divmod(nq - 2, unroll)

        def interior(first, count):
            for u in range(count):
                q0 = (first + u) * bq
                k0 = q0 - ATT_RADIUS
                if not isinstance(first, int):
                    q0, k0 = pl.multiple_of(q0, bq), pl.multiple_of(k0, ATT_RADIUS)
                for pi in range(pairs):
                    block(pi, q0, k0, (bias_ref[pi, 0], bias_ref[pi, 1]))

        def body(i, carry):
            interior(1 + i * unroll, unroll)
            return carry

        lax.fori_loop(0, trips, body, 0)
        interior(1 + trips * unroll, rest)
    if nq > 1:
        for pi in range(pairs):
            block(pi, n - bq, n - win, make_bias(pi, bq - win))


def band_attention(slopes, qkv):
    b, _, dil, n, _ = qkv.shape
    bq = min(ATT_QBLOCK, n)
    win = min(bq + 2 * ATT_RADIUS, n)
    pairs = max(1, min(N_SLABS, ATT_ROWS_PER_STEP // n))
    part = lambda c: pl.BlockSpec((1, pairs, 1, n, LANES), lambda bi, r, j, c=c: (bi, c * (N_SLABS // pairs) + j, r, 0, 0))
    out = pl.BlockSpec((1, pairs, 1, n, LANES), lambda bi, r, j: (bi, j, r, 0, 0))
    return pl.pallas_call(
        functools.partial(_band_kernel, n=n, bq=bq, win=win, dil=dil, pairs=pairs),
        out_shape=(jax.ShapeDtypeStruct((b, N_SLABS, dil, n, LANES), BF16),
                   jax.ShapeDtypeStruct((b, N_SLABS, dil, n, LANES), F32)),
        grid=(b, dil, N_SLABS // pairs),
        in_specs=[pl.BlockSpec(memory_space=pltpu.SMEM), part(0), part(1), part(2)],
        out_specs=(out, out),
        scratch_shapes=[pltpu.VMEM((pairs, 2, bq, win), F32)],
        compiler_params=_params("parallel", "parallel", "parallel"),
        name="band_attention",
    )(slopes, qkv, qkv, qkv)


def _trunk(x, mem, p):
    b, l, d = x.shape
    t = b * l
    for i in range(DEPTH):
        g = p["gains"][i]
        x = ffn_block(x.reshape(t, d), g[0], g[1], p["ffn_w_in"][i, 0], p["ffn_w_out"][i, 0]).reshape(b, l, d)
        j = i // 2
        if i % 2 == 0:
            nt = D_MODEL // MM_TILE_N
            qvg, gates = rms_matmul(x, g[2], p["hg_w_in"][j], [(3 * nt, BF16), (2 * nt, F32)], "planes")
            mix = hgrn_scan(qvg.reshape(b, -1, l, LANES), gates.reshape(b, -1, l, LANES),
                            p["lower_bounds"][i], p["hg_gnorm"][j])
            x = outproj_block(mix, p["hg_w_out"][j], g[3], x)
        else:
            nt = 3 * D_MODEL // MM_TILE_N
            res = [band_attention(p["slopes"], *rms_matmul(x, g[2], p["att_w_in"][j, gi], [(nt, BF16)], "planes", dil))
                   for gi, (_, dil) in enumerate(DIL_PATTERNS)]
            x = merge_outproj_block([o for o, _ in res], [lse for _, lse in res], p["att_w_out"][j], g[3], x)
        kv, = rms_matmul(mem, g[5], p["xa_w_kv"][i], [(2 * D_MODEL // MM_TILE_N, BF16)], "rows")
        x = xattn_block(x, kv, p["xa_w_q"][i], p["xa_w_o"][i], g[4], g[6])
        x = ffn_block(x.reshape(t, d), g[7], g[8], p["ffn_w_in"][i, 1], p["ffn_w_out"][i, 1]).reshape(b, l, d)
    return x


def kernel(x_prompt, x_sample, mem_prompt, mem_sample, norm_gains, ffn_w_in, ffn_w_out, hg_w_in, hg_lb_logits, hg_gnorm, hg_w_out, att_w_in, att_w_out, xa_w_q, xa_w_kv, xa_w_o):
    sm = jax.nn.softmax(hg_lb_logits.astype(F32), axis=0)
    lower_bounds = jnp.maximum(jnp.cumsum(sm, axis=0) - sm[0], 0.0)
    p = {
        "gains": norm_gains.astype(F32).reshape(DEPTH, -1, 1, D_MODEL),
        "ffn_w_in": col_tiles(ffn_w_in, FF_CHUNK),
        "ffn_w_out": ffn_w_out.astype(BF16).reshape(DEPTH, 2, D_FF // FF_CHUNK, FF_CHUNK, D_MODEL),
        "hg_w_in": col_tiles(jnp.concatenate([hg_w_in[..., s * D_MODEL:(s + 1) * D_MODEL] for s in (0, 3, 4, 1, 2)],
                                             axis=-1), MM_TILE_N),
        "lower_bounds": lower_bounds.reshape(DEPTH, HG_HEADS, 1, HG_DK),
        "hg_gnorm": hg_gnorm.astype(F32).reshape(-1, 1, HG_DK),
        "hg_w_out": hg_w_out.astype(BF16),
        "att_w_in": col_tiles(att_w_in.reshape(-1, D_MODEL, len(DIL_PATTERNS), 3 * D_MODEL).swapaxes(1, 2), MM_TILE_N),
        "att_w_out": att_w_out.astype(BF16),
        "xa_w_q": xa_w_q.astype(BF16),
        "xa_w_kv": col_tiles(xa_w_kv, MM_TILE_N),
        "xa_w_o": xa_w_o.astype(BF16),
        "slopes": jnp.exp2(-8.0 * jnp.arange(1, ATT_HEADS + 1, dtype=F32) / ATT_HEADS),
    }
    return (_trunk(x_prompt, mem_prompt, p), _trunk(x_sample, mem_sample, p))
```

```python
import functools

import jax
import jax.numpy as jnp
from jax import lax
from jax.experimental import pallas as pl
from jax.experimental.pallas import tpu as pltpu

F32 = jnp.float32
BF16 = jnp.bfloat16

D_MODEL = 1024
LANES = 128
N_SLABS = D_MODEL // LANES
DEPTH = 4
HG_HEADS = 8
HG_DK = 128
DIL_PATTERNS = ((128, 1), (512, 4), (2048, 16))
ATT_HEADS = 16
ATT_HD = 64
ATT_RADIUS = 64
X_HEADS = 4
X_HD = 256
D_FF = 2816
EPS = 1e-6
NEG = -1e30

HG_CHUNK = 64
HG_GROUP = 8
FF_CHUNK = 256
MM_TILE_N = 512
SUB_STRIDE = 4
ATT_QBLOCK = 128
ATT_INFLIGHT = 8
ATT_PHASE = 4
ATT_ROWS_PER_STEP = 4096
VMEM_LIMIT = 56 * 1024 * 1024

_NT = (((1,), (1,)), ((), ()))
_TN = (((0,), (0,)), ((), ()))


def _params(*sem):
    return pltpu.CompilerParams(dimension_semantics=sem, vmem_limit_bytes=VMEM_LIMIT)


def _rms(x, g):
    return x * lax.rsqrt(jnp.mean(x * x, axis=-1, keepdims=True) + EPS) * g


def _silu(x):
    return x / (1.0 + jnp.exp(-x))


def _row_tile(t, pref):
    while t % pref:
        pref //= 2
    return pref


def _slabs(ref):
    return jnp.concatenate([ref[0, c] for c in range(ref.shape[1])], axis=-1)


def _rms_matmul_kernel(x_ref, g_ref, w_ref, *refs, layout, dil, splits):
    o_refs, scratch = refs[:len(splits)], refs[len(splits):]
    h = _rms(x_ref[0], g_ref[...]).astype(BF16)
    tn = w_ref.shape[2]
    spt = tn // LANES
    for o_ref, (t0, nt) in zip(o_refs, splits):
        for j in range(nt):
            y = jnp.dot(h, w_ref[t0 + j], preferred_element_type=F32)
            if layout == "rows":
                o_ref[0, :, j * tn:(j + 1) * tn] = y.astype(o_ref.dtype)
            elif dil == 1:
                for c in range(spt):
                    o_ref[0, j * spt + c, 0] = y[:, c * LANES:(c + 1) * LANES].astype(o_ref.dtype)
            else:
                y_ref, *mid = scratch
                rows = y.shape[0] // dil
                d1 = SUB_STRIDE if dil > SUB_STRIDE else 1
                d2 = dil // d1
                for c in range(spt):
                    y_ref[c] = y[:, c * LANES:(c + 1) * LANES]
                    for q in range(d1):
                        if d1 == 1:
                            src = y_ref.at[c]
                        else:
                            mid[0][c, q] = y_ref[c, pl.ds(q, rows * d2, stride=d1), :]
                            src = mid[0].at[c, q]
                        for p in range(d2):
                            o_ref[0, j * spt + c, q + d1 * p] = src[pl.ds(p, rows, stride=d2), :].astype(o_ref.dtype)


def col_tiles(w, tn):
    *lead, d, n = w.shape
    return jnp.swapaxes(w.astype(BF16).reshape(*lead, d, n // tn, tn), -3, -2)


def rms_matmul(x, g, w_tiles, out_dtypes, layout, dil=1, tm=512):
    b, l, d = x.shape
    _, _, tn = w_tiles.shape
    tm = _row_tile(l, tm)
    out_shapes, out_specs, splits, t0 = [], [], [], 0
    for nt, dtype in out_dtypes:
        n = nt * tn
        if layout == "rows":
            out_shapes.append(jax.ShapeDtypeStruct((b, l, n), dtype))
            out_specs.append(pl.BlockSpec((1, tm, n), lambda bi, i: (bi, i, 0)))
        else:
            out_shapes.append(jax.ShapeDtypeStruct((b, n // LANES, dil, l // dil, LANES), dtype))
            out_specs.append(pl.BlockSpec((1, n // LANES, dil, tm // dil, LANES), lambda bi, i: (bi, 0, 0, i, 0)))
        splits.append((t0, nt))
        t0 += nt
    scratch = []
    if dil > 1:
        scratch.append(pltpu.VMEM((tn // LANES, tm, LANES), F32))
    if dil > SUB_STRIDE:
        scratch.append(pltpu.VMEM((tn // LANES, SUB_STRIDE, tm // SUB_STRIDE, LANES), F32))
    return pl.pallas_call(
        functools.partial(_rms_matmul_kernel, layout=layout, dil=dil, splits=tuple(splits)),
        out_shape=tuple(out_shapes),
        grid=(b, l // tm),
        in_specs=[
            pl.BlockSpec((1, tm, d), lambda bi, i: (bi, i, 0)),
            pl.BlockSpec((1, d), lambda bi, i: (0, 0)),
            pl.BlockSpec(w_tiles.shape, lambda bi, i: (0, 0, 0), pipeline_mode=pl.Buffered(1)),
        ],
        out_specs=tuple(out_specs),
        scratch_shapes=scratch,
        compiler_params=_params("parallel", "parallel"),
        name="rms_matmul",
    )(x, g, w_tiles)


def _ffn_kernel(x_ref, g0_ref, g1_ref, wi_ref, wo_ref, o_ref, acc_ref):
    nf = wo_ref.shape[0]
    h = _rms(x_ref[...], g0_ref[...]).astype(BF16)
    for c in range(nf):
        gate = jnp.dot(h, wi_ref[c], preferred_element_type=F32)
        up = jnp.dot(h, wi_ref[nf + c], preferred_element_type=F32)
        act = (_silu(gate) * up).astype(BF16)
        down = jnp.dot(act, wo_ref[c], preferred_element_type=F32)
        if c == 0:
            acc_ref[...] = down
        else:
            acc_ref[...] += down
    o_ref[...] = x_ref[...] + 0.5 * _rms(acc_ref[...], g1_ref[...])


def ffn_block(x, g0, g1, w_in, w_out, tm=512):
    t, d = x.shape
    tm = _row_tile(t, tm)
    row = pl.BlockSpec((tm, d), lambda i: (i, 0))
    gain = pl.BlockSpec((1, d), lambda i: (0, 0))
    resident = lambda w: pl.BlockSpec(w.shape, lambda i: (0, 0, 0), pipeline_mode=pl.Buffered(1))
    return pl.pallas_call(
        _ffn_kernel,
        out_shape=jax.ShapeDtypeStruct((t, d), F32),
        grid=(t // tm,),
        in_specs=[row, gain, gain, resident(w_in), resident(w_out)],
        out_specs=row,
        scratch_shapes=[pltpu.VMEM((tm, d), F32)],
        compiler_params=_params("parallel"),
        name="ffn_block",
    )(x, g0, g1, w_in, w_out)


def _token_rows(ref, buf=None, mid=None):
    dil, sub = ref.shape[2], ref.shape[3]
    if dil == 1:
        return jnp.concatenate([ref[0, c, 0] for c in range(N_SLABS)], axis=-1).astype(F32)
    d1 = SUB_STRIDE if dil > SUB_STRIDE else 1
    d2 = dil // d1
    for c in range(N_SLABS):
        for q in range(d1):
            dst = buf.at[c] if d1 == 1 else mid.at[c, q]
            for p in range(d2):
                dst[pl.ds(p, sub, stride=d2), :] = ref[0, c, q + d1 * p].astype(F32)
            if d1 > 1:
                buf[c, pl.ds(q, sub * d2, stride=d1), :] = mid[c, q]
    return jnp.concatenate([buf[c] for c in range(N_SLABS)], axis=-1)


def _merged_groups(o0_ref, o1_ref, o2_ref, l0_ref, l1_ref, l2_ref, bl1, bl2, bo1, bo2, ml2, mo2):
    l0, l1, l2 = _token_rows(l0_ref), _token_rows(l1_ref, bl1), _token_rows(l2_ref, bl2, ml2)
    m = jnp.maximum(jnp.maximum(l0, l1), l2)
    e0, e1, e2 = jnp.exp(l0 - m), jnp.exp(l1 - m), jnp.exp(l2 - m)
    num = e0 * _token_rows(o0_ref) + e1 * _token_rows(o1_ref, bo1) + e2 * _token_rows(o2_ref, bo2, mo2)
    return (num / (e0 + e1 + e2)).astype(BF16)


def _mixer_xattn_kernel(*refs, n_mix, xattn):
    mix_refs, refs = refs[:n_mix], refs[n_mix:]
    if n_mix:
        (wm_ref, gm_ref), refs = refs[:2], refs[2:]
    x_ref, refs = refs[0], refs[1:]
    if xattn:
        (k_ref, v_ref, wq_ref, wo_ref, gq_ref, go_ref), refs = refs[:6], refs[6:]
    o_ref, *scratch = refs
    x = x_ref[0]
    if n_mix:
        mixed = _slabs(mix_refs[0]) if n_mix == 1 else _merged_groups(*mix_refs, *scratch)
        x = x + _rms(jnp.dot(mixed, wm_ref[...], preferred_element_type=F32), gm_ref[...])
    if not xattn:
        o_ref[0] = x
        return
    q = jnp.dot(_rms(x, gq_ref[...]).astype(BF16), wq_ref[...], preferred_element_type=F32).astype(BF16)
    k = k_ref[0]
    v = v_ref[0]
    heads = []
    for h in range(X_HEADS):
        sl = slice(h * X_HD, (h + 1) * X_HD)
        s = lax.dot_general(q[:, sl], k[:, sl], _NT, preferred_element_type=F32) * (X_HD ** -0.5)
        p = jnp.exp(s - jnp.max(s, axis=-1, keepdims=True))
        denom = jnp.sum(p, axis=-1, keepdims=True)
        heads.append(jnp.dot(p.astype(BF16), v[:, sl], preferred_element_type=F32) / denom)
    o = jnp.concatenate(heads, axis=-1).astype(BF16)
    c = jnp.dot(o, wo_ref[...], preferred_element_type=F32)
    o_ref[0] = x + _rms(c, go_ref[...])


def mixer_xattn_block(x, mix=(), mix_params=(), xattn_params=()):
    b, l, d = x.shape
    merge = len(mix) > 1
    tm = _row_tile(l, 256 if merge else 512)
    row = pl.BlockSpec((1, tm, d), lambda bi, i: (bi, i, 0))
    full = pl.BlockSpec((d, d), lambda bi, i: (0, 0), pipeline_mode=pl.Buffered(1))
    gain = pl.BlockSpec((1, d), lambda bi, i: (0, 0))
    specs, args, scratch = [], [], []
    if merge:
        specs += [pl.BlockSpec((1, N_SLABS, a.shape[2], tm // a.shape[2], LANES), lambda bi, i: (bi, 0, 0, i, 0))
                  for a in mix]
        scratch = ([pltpu.VMEM((N_SLABS, tm, LANES), F32)] * 4
                   + [pltpu.VMEM((N_SLABS, SUB_STRIDE, tm // SUB_STRIDE, LANES), F32)] * 2)
    elif mix:
        specs += [pl.BlockSpec((1, N_SLABS, tm, LANES), lambda bi, i: (bi, 0, i, 0))]
    if mix:
        specs += [full, gain]
        args += [*mix, *mix_params]
    specs.append(row)
    args.append(x)
    if xattn_params:
        kv, layer, w_q, w_o, g_q, g_o = xattn_params
        m = kv.shape[2]
        specs += [pl.BlockSpec((None, 1, m, d), lambda bi, i: (layer, bi, 0, 0)),
                  pl.BlockSpec((None, 1, m, d), lambda bi, i: (layer, bi, 0, 1)),
                  full, full, gain, gain]
        args += [kv, kv, w_q, w_o, g_q, g_o]
    return pl.pallas_call(
        functools.partial(_mixer_xattn_kernel, n_mix=len(mix), xattn=bool(xattn_params)),
        out_shape=jax.ShapeDtypeStruct((b, l, d), F32),
        grid=(b, l // tm),
        in_specs=specs,
        out_specs=row,
        scratch_shapes=scratch,
        compiler_params=_params("parallel", "parallel"),
        name="mixer_xattn_block",
    )(*args)


def kv_projection(mem, gains, w_tiles):
    b, m, d = mem.shape
    depth, nt, _, tn = w_tiles.shape
    return pl.pallas_call(
        functools.partial(_rms_matmul_kernel, layout="rows", dil=1, splits=((0, nt),)),
        out_shape=jax.ShapeDtypeStruct((depth, b, m, nt * tn), BF16),
        grid=(depth, b),
        in_specs=[
            pl.BlockSpec((1, m, d), lambda li, bi: (bi, 0, 0)),
            pl.BlockSpec((None, 1, d), lambda li, bi: (li, 0, 0)),
            pl.BlockSpec((None, nt, d, tn), lambda li, bi: (li, 0, 0, 0)),
        ],
        out_specs=pl.BlockSpec((None, 1, m, nt * tn), lambda li, bi: (li, bi, 0, 0)),
        compiler_params=_params("parallel", "parallel"),
        name="kv_projection",
    )(mem, gains, w_tiles)


def _hgrn_kernel(q_ref, zf_ref, zb_ref, v_ref, g_ref, lb_ref, gn_ref, o_ref, of_ref, ob_ref, stf_ref, stb_ref, *, seq):
    c_len = HG_CHUNK
    nc = seq // c_len
    half = c_len // 2
    lb = lb_ref[0]
    log_lb = jnp.log(lb)
    log_1m_lb = jnp.log1p(-lb)
    row = lax.broadcasted_iota(jnp.int32, (c_len, c_len), 0)
    col = lax.broadcasted_iota(jnp.int32, (c_len, c_len), 1)
    same_half = (row < half) == (col < half)
    lower = (col <= row) & same_half
    upper = (col >= row) & same_half
    lower_sum = (col <= row).astype(F32)
    upper_sum = (col >= row).astype(F32)

    def halves(first, second):
        return jnp.concatenate([jnp.broadcast_to(first, (half, HG_DK)), jnp.broadcast_to(second, (half, HG_DK))], axis=0)

    def gate(z):
        e = jnp.exp(-jnp.abs(z))
        log_sig = jnp.minimum(z, 0.0) - jnp.log(1.0 + e)
        t = log_1m_lb + log_sig
        logf = jnp.maximum(log_lb, t) + jnp.log(1.0 + jnp.exp(-jnp.abs(log_lb - t)))
        sig_neg = jnp.where(z >= 0.0, e, 1.0) / (1.0 + e)
        return logf, (1.0 - lb) * sig_neg

    grp = _row_tile(nc, HG_GROUP)
    ng = nc // grp
    quarter = half // 2
    scans = ((zf_ref, stf_ref, of_ref, lower_sum, lower, (quarter - 1, half - 1, half + quarter - 1, c_len - 1), False),
             (zb_ref, stb_ref, ob_ref, upper_sum, upper, (quarter, half, half + quarter, 0), True))

    stf_ref[...] = jnp.zeros_like(stf_ref)
    stb_ref[...] = jnp.zeros_like(stb_ref)

    def scan(i, carry):
        steps = []
        for z_ref, st_ref, out_ref, sum_mat, causal, ref_rows, rev in scans:
            g0 = (ng - 1 - i) if rev else i
            sls = [pl.ds(pl.multiple_of((g0 * grp + j) * c_len, c_len), c_len) for j in range(grp)]
            gates = [gate(z_ref[0, 0, sl, :]) for sl in sls]
            cums = jnp.dot(sum_mat, jnp.concatenate([lf for lf, _ in gates], axis=1),
                           precision=lax.Precision.HIGHEST, preferred_element_type=F32)
            for j in (reversed(range(grp)) if rev else range(grp)):
                steps.append(dict(sl=sls[j], k=gates[j][1], cum=cums[:, j * HG_DK:(j + 1) * HG_DK], causal=causal,
                                  ref_rows=ref_rows, rev=rev, st_ref=st_ref, out_ref=out_ref))
        order = [steps[d * grp + j] for j in range(grp) for d in range(2)]
        zeros = jnp.zeros((half, HG_DK), F32)
        for e in order:
            cum = e["cum"]
            m_a, m_mid, m_b, tot = (cum[r:r + 1, :] for r in e["ref_rows"])
            q = _silu(q_ref[0, 0, e["sl"], :].astype(F32))
            e["v"] = v_ref[0, 0, e["sl"], :]
            ref = halves(m_a, m_b)
            qd = q * jnp.exp(cum - ref)
            kd = e["k"] * jnp.exp(ref - cum)
            e["qe"] = (qd * halves(jnp.exp(m_a), jnp.exp(m_b))).astype(BF16)
            ke = (kd * halves(jnp.exp(tot - m_a), jnp.exp(tot - m_b))).astype(BF16)
            e["decay"] = jnp.exp(tot)
            if e["rev"]:
                qx = jnp.concatenate([qd[:half] * jnp.exp(m_a - m_mid), zeros], axis=0)
                kx = jnp.concatenate([zeros, kd[half:] * jnp.exp(m_mid - m_b)], axis=0)
            else:
                qx = jnp.concatenate([zeros, qd[half:] * jnp.exp(m_b - m_mid)], axis=0)
                kx = jnp.concatenate([kd[:half] * jnp.exp(m_mid - m_a), zeros], axis=0)
            e["s_in"] = lax.dot_general(qd.astype(BF16), kd.astype(BF16), _NT, preferred_element_type=F32)
            e["s_x"] = lax.dot_general(qx.astype(BF16), kx.astype(BF16), _NT, preferred_element_type=F32)
            e["upd"] = lax.dot_general(e["v"], ke, _TN, preferred_element_type=F32)
        for e in order:
            s = (jnp.where(e["causal"], e["s_in"], 0.0) + e["s_x"]).astype(BF16)
            e["o"] = jnp.dot(s, e["v"], preferred_element_type=F32)
        states = [st_ref[...] for _, st_ref, *_ in scans]
        for idx, e in enumerate(order):
            st = states[idx % 2]
            o = e["o"] + lax.dot_general(e["qe"], st.astype(BF16), _NT, preferred_element_type=F32)
            e["out_ref"][e["sl"], :] = o
            states[idx % 2] = st * e["decay"] + e["upd"]
        for st, (_, st_ref, *_) in zip(states, scans):
            st_ref[...] = st
        return carry

    lax.fori_loop(0, ng, scan, 0)

    rows = _row_tile(seq, 256)

    def finish(i, carry):
        sl = pl.ds(pl.multiple_of(i * rows, rows), rows)
        o = of_ref[sl, :] + ob_ref[sl, :]
        o = o * lax.rsqrt(jnp.mean(o * o, axis=-1, keepdims=True) + EPS)
        o_ref[0, 0, sl, :] = (o * gn_ref[...] * _silu(g_ref[0, 0, sl, :].astype(F32))).astype(o_ref.dtype)
        return carry

    lax.fori_loop(0, seq // rows, finish, 0)


def hgrn_scan(qvg, gates, lb, gnorm):
    b, _, l, _ = qvg.shape
    part = lambda p: pl.BlockSpec((1, 1, l, HG_DK), lambda i, h, p=p: (i, p * HG_HEADS + h, 0, 0))
    return pl.pallas_call(
        functools.partial(_hgrn_kernel, seq=l),
        out_shape=jax.ShapeDtypeStruct((b, HG_HEADS, l, HG_DK), BF16),
        grid=(b, HG_HEADS),
        in_specs=[part(0), part(0), part(1), part(1), part(2),
                  pl.BlockSpec((1, 1, HG_DK), lambda i, h: (h, 0, 0)),
                  pl.BlockSpec((1, HG_DK), lambda i, h: (0, 0))],
        out_specs=pl.BlockSpec((1, 1, l, HG_DK), lambda i, h: (i, h, 0, 0)),
        scratch_shapes=[pltpu.VMEM((l, HG_DK), F32), pltpu.VMEM((l, HG_DK), F32),
                        pltpu.VMEM((HG_DK, HG_DK), F32), pltpu.VMEM((HG_DK, HG_DK), F32)],
        compiler_params=_params("parallel", "parallel"),
        name="hgrn_scan",
    )(qvg, gates, gates, qvg, qvg, lb, gnorm)


def _band_kernel(slope_ref, q_ref, k_ref, v_ref, o_ref, l_ref, bias_ref, *, n, bq, win, dil, pairs):
    nq = n // bq
    lane = lax.broadcasted_iota(jnp.int32, (1, LANES), 1)
    head0 = lane < ATT_HD
    qsel = (jnp.where(head0, ATT_HD ** -0.5, 0.0).astype(BF16), jnp.where(head0, 0.0, ATT_HD ** -0.5).astype(BF16))
    ones0 = jnp.broadcast_to(jnp.where(head0, 1.0, 0.0).astype(BF16), (win, LANES))
    ones1 = jnp.broadcast_to(jnp.where(head0, 0.0, 1.0).astype(BF16), (win, LANES))
    delta = (lax.broadcasted_iota(jnp.int32, (bq, win), 1) - lax.broadcasted_iota(jnp.int32, (bq, win), 0))

    def make_bias(pi, offset):
        rel = jnp.abs(delta + offset)
        base = jnp.where(rel <= ATT_RADIUS, -(dil * rel).astype(F32), NEG)
        pair = pl.program_id(2) * pairs + pi
        return [slope_ref[2 * pair + h] * base for h in range(2)]

    def scores(pi, q0, k0, bias):
        q = q_ref[0, pi, 0, pl.ds(q0, bq), :]
        k = k_ref[0, pi, 0, pl.ds(k0, win), :]
        ps, ms = [], []
        for h in range(2):
            s = lax.dot_general(q * qsel[h], k, _NT, preferred_element_type=F32) + bias[h]
            m = jnp.max(s, axis=-1, keepdims=True)
            ps.append(jnp.exp(s - m).astype(BF16))
            ms.append(m)
        return pi, q0, k0, ps, ms

    def finish(pi, q0, k0, ps, ms):
        v = v_ref[0, pi, 0, pl.ds(k0, win), :]
        rhs = jnp.concatenate([
            jnp.concatenate([jnp.where(head0, v, 0), ones0], axis=1),
            jnp.concatenate([jnp.where(head0, 0, v), ones1], axis=1)], axis=0)
        r = jnp.dot(jnp.concatenate(ps, axis=1), rhs, preferred_element_type=F32)
        den = r[:, LANES:]
        o_ref[0, pi, 0, pl.ds(q0, bq), :] = (r[:, :LANES] / den).astype(o_ref.dtype)
        l_ref[0, pi, 0, pl.ds(q0, bq), :] = jnp.where(head0, ms[0], ms[1]) + jnp.log(den)

    def blocks(todo):
        for i in range(0, len(todo), ATT_PHASE):
            for part in [scores(*t) for t in todo[i:i + ATT_PHASE]]:
                finish(*part)

    blocks([(pi, 0, 0, make_bias(pi, 0)) for pi in range(pairs)])
    if nq > 2:
        for pi in range(pairs):
            b0, b1 = make_bias(pi, -ATT_RADIUS)
            bias_ref[pi, 0] = b0
            bias_ref[pi, 1] = b1
        unroll = max(1, ATT_INFLIGHT // pairs)
        trips, rest = divmod(nq - 2, unroll)

        def interior(first, count):
            todo = []
            for u in range(count):
                q0 = (first + u) * bq
                k0 = q0 - ATT_RADIUS
                if not isinstance(first, int):
                    q0, k0 = pl.multiple_of(q0, bq), pl.multiple_of(k0, ATT_RADIUS)
                todo += [(pi, q0, k0, (bias_ref[pi, 0], bias_ref[pi, 1])) for pi in range(pairs)]
            blocks(todo)

        def body(i, carry):
            interior(1 + i * unroll, unroll)
            return carry

        lax.fori_loop(0, trips, body, 0)
        interior(1 + trips * unroll, rest)
    if nq > 1:
        blocks([(pi, n - bq, n - win, make_bias(pi, bq - win)) for pi in range(pairs)])


def band_attention(slopes, qkv):
    b, _, dil, n, _ = qkv.shape
    bq = min(ATT_QBLOCK, n)
    win = min(bq + 2 * ATT_RADIUS, n)
    pairs = max(1, min(N_SLABS, ATT_ROWS_PER_STEP // n))
    part = lambda c: pl.BlockSpec((1, pairs, 1, n, LANES), lambda bi, r, j, c=c: (bi, c * (N_SLABS // pairs) + j, r, 0, 0))
    out = pl.BlockSpec((1, pairs, 1, n, LANES), lambda bi, r, j: (bi, j, r, 0, 0))
    return pl.pallas_call(
        functools.partial(_band_kernel, n=n, bq=bq, win=win, dil=dil, pairs=pairs),
        out_shape=(jax.ShapeDtypeStruct((b, N_SLABS, dil, n, LANES), BF16),
                   jax.ShapeDtypeStruct((b, N_SLABS, dil, n, LANES), F32)),
        grid=(b, dil, N_SLABS // pairs),
        in_specs=[pl.BlockSpec(memory_space=pltpu.SMEM), part(0), part(1), part(2)],
        out_specs=(out, out),
        scratch_shapes=[pltpu.VMEM((pairs, 2, bq, win), F32)],
        compiler_params=_params("parallel", "parallel", "parallel"),
        name="band_attention",
    )(slopes, qkv, qkv, qkv)


def _trunk(x, mem, p):
    b, l, d = x.shape
    t = b * l
    kv = kv_projection(mem, p["gains"][:, 5], p["xa_w_kv"])
    for i in range(DEPTH):
        g = p["gains"][i]
        xattn = (kv, i, p["xa_w_q"][i], p["xa_w_o"][i], g[4], g[6])
        x = ffn_block(x.reshape(t, d), g[0], g[1], p["ffn_w_in"][i, 0], p["ffn_w_out"][i, 0]).reshape(b, l, d)
        j = i // 2
        if i % 2 == 0:
            nt = D_MODEL // MM_TILE_N
            qvg, gates = rms_matmul(x, g[2], p["hg_w_in"][j], [(3 * nt, BF16), (2 * nt, F32)], "planes")
            mix = [hgrn_scan(qvg.reshape(b, -1, l, LANES), gates.reshape(b, -1, l, LANES),
                             p["lower_bounds"][i], p["hg_gnorm"][j])]
            x = mixer_xattn_block(x, mix, (p["hg_w_out"][j], g[3]), xattn)
        else:
            nt = 3 * D_MODEL // MM_TILE_N
            res = [band_attention(p["slopes"], *rms_matmul(x, g[2], p["att_w_in"][j, gi], [(nt, BF16)], "planes", dil))
                   for gi, (_, dil) in enumerate(DIL_PATTERNS)]
            mix = [o for o, _ in res] + [lse for _, lse in res]
            x = mixer_xattn_block(x, mix, (p["att_w_out"][j], g[3]))
            x = mixer_xattn_block(x, xattn_params=xattn)
        x = ffn_block(x.reshape(t, d), g[7], g[8], p["ffn_w_in"][i, 1], p["ffn_w_out"][i, 1]).reshape(b, l, d)
    return x


def kernel(x_prompt, x_sample, mem_prompt, mem_sample, norm_gains, ffn_w_in, ffn_w_out, hg_w_in, hg_lb_logits, hg_gnorm, hg_w_out, att_w_in, att_w_out, xa_w_q, xa_w_kv, xa_w_o):
    sm = jax.nn.softmax(hg_lb_logits.astype(F32), axis=0)
    lower_bounds = jnp.maximum(jnp.cumsum(sm, axis=0) - sm[0], 0.0)
    p = {
        "gains": norm_gains.astype(F32).reshape(DEPTH, -1, 1, D_MODEL),
        "ffn_w_in": col_tiles(ffn_w_in, FF_CHUNK),
        "ffn_w_out": ffn_w_out.astype(BF16).reshape(DEPTH, 2, D_FF // FF_CHUNK, FF_CHUNK, D_MODEL),
        "hg_w_in": col_tiles(jnp.concatenate([hg_w_in[..., s * D_MODEL:(s + 1) * D_MODEL] for s in (0, 3, 4, 1, 2)],
                                             axis=-1), MM_TILE_N),
        "lower_bounds": lower_bounds.reshape(DEPTH, HG_HEADS, 1, HG_DK),
        "hg_gnorm": hg_gnorm.astype(F32).reshape(-1, 1, HG_DK),
        "hg_w_out": hg_w_out.astype(BF16),
        "att_w_in": col_tiles(att_w_in.reshape(-1, D_MODEL, len(DIL_PATTERNS), 3 * D_MODEL).swapaxes(1, 2), MM_TILE_N),
        "att_w_out": att_w_out.astype(BF16),
        "xa_w_q": xa_w_q.astype(BF16),
        "xa_w_kv": col_tiles(xa_w_kv, MM_TILE_N),
        "xa_w_o": xa_w_o.astype(BF16),
        "slopes": jnp.exp2(-8.0 * jnp.arange(1, ATT_HEADS + 1, dtype=F32) / ATT_HEADS),
    }
    return (_trunk(x_prompt, mem_prompt, p), _trunk(x_sample, mem_sample, p))
```

```python
import functools

import jax
import jax.numpy as jnp
from jax import lax
from jax.experimental import pallas as pl
from jax.experimental.pallas import tpu as pltpu

F32 = jnp.float32
BF16 = jnp.bfloat16

D_MODEL = 1024
LANES = 128
N_SLABS = D_MODEL // LANES
DEPTH = 4
HG_HEADS = 8
HG_DK = 128
DIL_PATTERNS = ((128, 1), (512, 4), (2048, 16))
ATT_HEADS = 16
ATT_HD = 64
ATT_RADIUS = 64
X_HEADS = 4
X_HD = 256
D_FF = 2816
EPS = 1e-6
NEG = -1e30

HG_CHUNK = 64
HG_GROUP = 16
FF_CHUNK = 256
MM_TILE_N = 1024
SUB_STRIDE = 4
ATT_QBLOCK = 128
ATT_INFLIGHT = 16
ATT_PHASE = 4
ATT_ROWS_PER_STEP = 4096
VMEM_LIMIT = 56 * 1024 * 1024

_NT = (((1,), (1,)), ((), ()))
_TN = (((0,), (0,)), ((), ()))


def _params(*sem):
    return pltpu.CompilerParams(dimension_semantics=sem, vmem_limit_bytes=VMEM_LIMIT)


def _rms(x, g):
    return x * lax.rsqrt(jnp.mean(x * x, axis=-1, keepdims=True) + EPS) * g


def _silu(x):
    return x / (1.0 + jnp.exp(-x))


def _row_tile(t, pref):
    while t % pref:
        pref //= 2
    return pref


def _slabs(ref):
    return jnp.concatenate([ref[0, c] for c in range(ref.shape[1])], axis=-1)


def _rms_matmul_kernel(x_ref, g_ref, w_ref, *refs, layout, dil, splits):
    o_refs, scratch = refs[:len(splits)], refs[len(splits):]
    h = _rms(x_ref[0], g_ref[...]).astype(BF16)
    tn = w_ref.shape[2]
    spt = tn // LANES
    for o_ref, (t0, nt) in zip(o_refs, splits):
        for j in range(nt):
            y = jnp.dot(h, w_ref[t0 + j], preferred_element_type=F32)
            if layout == "rows":
                o_ref[0, :, j * tn:(j + 1) * tn] = y.astype(o_ref.dtype)
            elif dil == 1:
                for c in range(spt):
                    o_ref[0, j * spt + c, 0] = y[:, c * LANES:(c + 1) * LANES].astype(o_ref.dtype)
            else:
                y_ref, *mid = scratch
                rows = y.shape[0] // dil
                d1 = SUB_STRIDE if dil > SUB_STRIDE else 1
                d2 = dil // d1
                for c in range(spt):
                    y_ref[c] = y[:, c * LANES:(c + 1) * LANES]
                    for q in range(d1):
                        if d1 == 1:
                            src = y_ref.at[c]
                        else:
                            mid[0][c, q] = y_ref[c, pl.ds(q, rows * d2, stride=d1), :]
                            src = mid[0].at[c, q]
                        for p in range(d2):
                            o_ref[0, j * spt + c, q + d1 * p] = src[pl.ds(p, rows, stride=d2), :].astype(o_ref.dtype)


def col_tiles(w, tn):
    *lead, d, n = w.shape
    return jnp.swapaxes(w.astype(BF16).reshape(*lead, d, n // tn, tn), -3, -2)


def rms_matmul(x, g, w_tiles, out_dtypes, layout, dil=1, tm=512):
    b, l, d = x.shape
    _, _, tn = w_tiles.shape
    tm = _row_tile(l, tm)
    out_shapes, out_specs, splits, t0 = [], [], [], 0
    for nt, dtype in out_dtypes:
        n = nt * tn
        if layout == "rows":
            out_shapes.append(jax.ShapeDtypeStruct((b, l, n), dtype))
            out_specs.append(pl.BlockSpec((1, tm, n), lambda bi, i: (bi, i, 0)))
        else:
            out_shapes.append(jax.ShapeDtypeStruct((b, n // LANES, dil, l // dil, LANES), dtype))
            out_specs.append(pl.BlockSpec((1, n // LANES, dil, tm // dil, LANES), lambda bi, i: (bi, 0, 0, i, 0)))
        splits.append((t0, nt))
        t0 += nt
    scratch = []
    if dil > 1:
        scratch.append(pltpu.VMEM((tn // LANES, tm, LANES), F32))
    if dil > SUB_STRIDE:
        scratch.append(pltpu.VMEM((tn // LANES, SUB_STRIDE, tm // SUB_STRIDE, LANES), F32))
    return pl.pallas_call(
        functools.partial(_rms_matmul_kernel, layout=layout, dil=dil, splits=tuple(splits)),
        out_shape=tuple(out_shapes),
        grid=(b, l // tm),
        in_specs=[
            pl.BlockSpec((1, tm, d), lambda bi, i: (bi, i, 0)),
            pl.BlockSpec((1, d), lambda bi, i: (0, 0)),
            pl.BlockSpec(w_tiles.shape, lambda bi, i: (0, 0, 0), pipeline_mode=pl.Buffered(1)),
        ],
        out_specs=tuple(out_specs),
        scratch_shapes=scratch,
        compiler_params=_params("parallel", "parallel"),
        name="rms_matmul",
    )(x, g, w_tiles)


def _ffn_kernel(x_ref, g0_ref, g1_ref, wi_ref, wo_ref, o_ref, acc_ref):
    nf = wo_ref.shape[0]
    h = _rms(x_ref[...], g0_ref[...]).astype(BF16)
    for c in range(nf):
        gate = jnp.dot(h, wi_ref[c], preferred_element_type=F32)
        up = jnp.dot(h, wi_ref[nf + c], preferred_element_type=F32)
        act = (_silu(gate) * up).astype(BF16)
        down = jnp.dot(act, wo_ref[c], preferred_element_type=F32)
        if c == 0:
            acc_ref[...] = down
        else:
            acc_ref[...] += down
    o_ref[...] = x_ref[...] + 0.5 * _rms(acc_ref[...], g1_ref[...])


def ffn_block(x, g0, g1, w_in, w_out, tm=512):
    t, d = x.shape
    tm = _row_tile(t, tm)
    row = pl.BlockSpec((tm, d), lambda i: (i, 0))
    gain = pl.BlockSpec((1, d), lambda i: (0, 0))
    resident = lambda w: pl.BlockSpec(w.shape, lambda i: (0, 0, 0), pipeline_mode=pl.Buffered(1))
    return pl.pallas_call(
        _ffn_kernel,
        out_shape=jax.ShapeDtypeStruct((t, d), F32),
        grid=(t // tm,),
        in_specs=[row, gain, gain, resident(w_in), resident(w_out)],
        out_specs=row,
        scratch_shapes=[pltpu.VMEM((tm, d), F32)],
        compiler_params=_params("parallel"),
        name="ffn_block",
    )(x, g0, g1, w_in, w_out)


def _token_rows(ref, buf=None, mid=None):
    dil, sub = ref.shape[2], ref.shape[3]
    if dil == 1:
        return jnp.concatenate([ref[0, c, 0] for c in range(N_SLABS)], axis=-1).astype(F32)
    d1 = SUB_STRIDE if dil > SUB_STRIDE else 1
    d2 = dil // d1
    for c in range(N_SLABS):
        for q in range(d1):
            dst = buf.at[c] if d1 == 1 else mid.at[c, q]
            for p in range(d2):
                dst[pl.ds(p, sub, stride=d2), :] = ref[0, c, q + d1 * p].astype(F32)
            if d1 > 1:
                buf[c, pl.ds(q, sub * d2, stride=d1), :] = mid[c, q]
    return jnp.concatenate([buf[c] for c in range(N_SLABS)], axis=-1)


def _merged_groups(o0_ref, o1_ref, o2_ref, l0_ref, l1_ref, l2_ref, bl1, bl2, bo1, bo2, ml2, mo2):
    l0, l1, l2 = _token_rows(l0_ref), _token_rows(l1_ref, bl1), _token_rows(l2_ref, bl2, ml2)
    m = jnp.maximum(jnp.maximum(l0, l1), l2)
    e0, e1, e2 = jnp.exp(l0 - m), jnp.exp(l1 - m), jnp.exp(l2 - m)
    num = e0 * _token_rows(o0_ref) + e1 * _token_rows(o1_ref, bo1) + e2 * _token_rows(o2_ref, bo2, mo2)
    return (num / (e0 + e1 + e2)).astype(BF16)


def _mixer_xattn_kernel(*refs, n_mix, xattn):
    mix_refs, refs = refs[:n_mix], refs[n_mix:]
    if n_mix:
        (wm_ref, gm_ref), refs = refs[:2], refs[2:]
    x_ref, refs = refs[0], refs[1:]
    if xattn:
        (k_ref, v_ref, wq_ref, wo_ref, gq_ref, go_ref), refs = refs[:6], refs[6:]
    o_ref, *scratch = refs
    x = x_ref[0]
    if n_mix:
        mixed = _slabs(mix_refs[0]) if n_mix == 1 else _merged_groups(*mix_refs, *scratch)
        x = x + _rms(jnp.dot(mixed, wm_ref[...], preferred_element_type=F32), gm_ref[...])
    if not xattn:
        o_ref[0] = x
        return
    q = jnp.dot(_rms(x, gq_ref[...]).astype(BF16), wq_ref[...], preferred_element_type=F32).astype(BF16)
    k = k_ref[0]
    v = v_ref[0]
    cols = [slice(hd * X_HD, (hd + 1) * X_HD) for hd in range(X_HEADS)]
    scores = [lax.dot_general(q[:, sl], k[:, sl], _NT, preferred_element_type=F32) * (X_HD ** -0.5) for sl in cols]
    probs = [jnp.exp(s - jnp.max(s, axis=-1, keepdims=True)) for s in scores]
    heads = [jnp.dot(p.astype(BF16), v[:, sl], preferred_element_type=F32) / jnp.sum(p, axis=-1, keepdims=True)
             for p, sl in zip(probs, cols)]
    c = jnp.dot(jnp.concatenate(heads, axis=-1).astype(BF16), wo_ref[...], preferred_element_type=F32)
    o_ref[0] = x + _rms(c, go_ref[...])


def mixer_xattn_block(x, mix=(), mix_params=(), xattn_params=()):
    b, l, d = x.shape
    merge = len(mix) > 1
    tm = _row_tile(l, 256 if merge else 512)
    row = pl.BlockSpec((1, tm, d), lambda bi, i: (bi, i, 0))
    full = pl.BlockSpec((d, d), lambda bi, i: (0, 0), pipeline_mode=pl.Buffered(1))
    gain = pl.BlockSpec((1, d), lambda bi, i: (0, 0))
    specs, args, scratch = [], [], []
    if merge:
        specs += [pl.BlockSpec((1, N_SLABS, a.shape[2], tm // a.shape[2], LANES), lambda bi, i: (bi, 0, 0, i, 0))
                  for a in mix]
        scratch = ([pltpu.VMEM((N_SLABS, tm, LANES), F32)] * 4
                   + [pltpu.VMEM((N_SLABS, SUB_STRIDE, tm // SUB_STRIDE, LANES), F32)] * 2)
    elif mix:
        specs += [pl.BlockSpec((1, N_SLABS, tm, LANES), lambda bi, i: (bi, 0, i, 0))]
    if mix:
        specs += [full, gain]
        args += [*mix, *mix_params]
    specs.append(row)
    args.append(x)
    if xattn_params:
        kv, layer, w_q, w_o, g_q, g_o = xattn_params
        m = kv.shape[2]
        specs += [pl.BlockSpec((None, 1, m, d), lambda bi, i: (layer, bi, 0, 0)),
                  pl.BlockSpec((None, 1, m, d), lambda bi, i: (layer, bi, 0, 1)),
                  full, full, gain, gain]
        args += [kv, kv, w_q, w_o, g_q, g_o]
    return pl.pallas_call(
        functools.partial(_mixer_xattn_kernel, n_mix=len(mix), xattn=bool(xattn_params)),
        out_shape=jax.ShapeDtypeStruct((b, l, d), F32),
        grid=(b, l // tm),
        in_specs=specs,
        out_specs=row,
        scratch_shapes=scratch,
        compiler_params=_params("parallel", "parallel"),
        name="mixer_xattn_block",
    )(*args)


def kv_projection(mem, gains, w_tiles):
    b, m, d = mem.shape
    depth, nt, _, tn = w_tiles.shape
    return pl.pallas_call(
        functools.partial(_rms_matmul_kernel, layout="rows", dil=1, splits=((0, nt),)),
        out_shape=jax.ShapeDtypeStruct((depth, b, m, nt * tn), BF16),
        grid=(depth, b),
        in_specs=[
            pl.BlockSpec((1, m, d), lambda li, bi: (bi, 0, 0)),
            pl.BlockSpec((None, 1, d), lambda li, bi: (li, 0, 0)),
            pl.BlockSpec((None, nt, d, tn), lambda li, bi: (li, 0, 0, 0)),
        ],
        out_specs=pl.BlockSpec((None, 1, m, nt * tn), lambda li, bi: (li, bi, 0, 0)),
        compiler_params=_params("parallel", "parallel"),
        name="kv_projection",
    )(mem, gains, w_tiles)


def _hgrn_kernel(q_ref, zf_ref, zb_ref, v_ref, g_ref, lb_ref, gn_ref, o_ref, of_ref, ob_ref, stf_ref, stb_ref, *, seq):
    c_len = HG_CHUNK
    nc = seq // c_len
    half = c_len // 2
    lb = lb_ref[0]
    log_lb = jnp.log(lb)
    log_1m_lb = jnp.log1p(-lb)
    row = lax.broadcasted_iota(jnp.int32, (c_len, c_len), 0)
    col = lax.broadcasted_iota(jnp.int32, (c_len, c_len), 1)
    same_half = (row < half) == (col < half)
    lower = (col <= row) & same_half
    upper = (col >= row) & same_half
    lower_sum = (col <= row).astype(F32)
    upper_sum = (col >= row).astype(F32)

    def halves(first, second):
        return jnp.concatenate([jnp.broadcast_to(first, (half, HG_DK)), jnp.broadcast_to(second, (half, HG_DK))], axis=0)

    def gate(z):
        e = jnp.exp(-jnp.abs(z))
        log_sig = jnp.minimum(z, 0.0) - jnp.log(1.0 + e)
        t = log_1m_lb + log_sig
        logf = jnp.maximum(log_lb, t) + jnp.log(1.0 + jnp.exp(-jnp.abs(log_lb - t)))
        sig_neg = jnp.where(z >= 0.0, e, 1.0) / (1.0 + e)
        return logf, (1.0 - lb) * sig_neg

    grp = _row_tile(nc, HG_GROUP)
    ng = nc // grp
    quarter = half // 2
    scans = ((zf_ref, stf_ref, of_ref, lower_sum, lower, (quarter - 1, half - 1, half + quarter - 1, c_len - 1), False),
             (zb_ref, stb_ref, ob_ref, upper_sum, upper, (quarter, half, half + quarter, 0), True))

    stf_ref[...] = jnp.zeros_like(stf_ref)
    stb_ref[...] = jnp.zeros_like(stb_ref)

    def scan(i, carry):
        steps = []
        for z_ref, st_ref, out_ref, sum_mat, causal, ref_rows, rev in scans:
            g0 = (ng - 1 - i) if rev else i
            sls = [pl.ds(pl.multiple_of((g0 * grp + j) * c_len, c_len), c_len) for j in range(grp)]
            gates = [gate(z_ref[0, 0, sl, :]) for sl in sls]
            cums = jnp.dot(sum_mat, jnp.concatenate([lf for lf, _ in gates], axis=1),
                           precision=lax.Precision.HIGHEST, preferred_element_type=F32)
            for j in (reversed(range(grp)) if rev else range(grp)):
                steps.append(dict(sl=sls[j], k=gates[j][1], cum=cums[:, j * HG_DK:(j + 1) * HG_DK], causal=causal,
                                  ref_rows=ref_rows, rev=rev, st_ref=st_ref, out_ref=out_ref))
        order = [steps[d * grp + j] for j in range(grp) for d in range(2)]
        zeros = jnp.zeros((half, HG_DK), F32)
        for e in order:
            cum = e["cum"]
            m_a, m_mid, m_b, tot = (cum[r:r + 1, :] for r in e["ref_rows"])
            q = _silu(q_ref[0, 0, e["sl"], :].astype(F32))
            e["v"] = v_ref[0, 0, e["sl"], :]
            ref = halves(m_a, m_b)
            qd = q * jnp.exp(cum - ref)
            kd = e["k"] * jnp.exp(ref - cum)
            e["qe"] = (qd * halves(jnp.exp(m_a), jnp.exp(m_b))).astype(BF16)
            ke = (kd * halves(jnp.exp(tot - m_a), jnp.exp(tot - m_b))).astype(BF16)
            e["decay"] = jnp.exp(tot)
            if e["rev"]:
                qx = jnp.concatenate([qd[:half] * jnp.exp(m_a - m_mid), zeros], axis=0)
                kx = jnp.concatenate([zeros, kd[half:] * jnp.exp(m_mid - m_b)], axis=0)
            else:
                qx = jnp.concatenate([zeros, qd[half:] * jnp.exp(m_b - m_mid)], axis=0)
                kx = jnp.concatenate([kd[:half] * jnp.exp(m_mid - m_a), zeros], axis=0)
            e["s_in"] = lax.dot_general(qd.astype(BF16), kd.astype(BF16), _NT, preferred_element_type=F32)
            e["s_x"] = lax.dot_general(qx.astype(BF16), kx.astype(BF16), _NT, preferred_element_type=F32)
            e["upd"] = lax.dot_general(e["v"], ke, _TN, preferred_element_type=F32)
        for e in order:
            s = (jnp.where(e["causal"], e["s_in"], 0.0) + e["s_x"]).astype(BF16)
            e["o"] = jnp.dot(s, e["v"], preferred_element_type=F32)
        states = [st_ref[...] for _, st_ref, *_ in scans]
        for idx, e in enumerate(order):
            st = states[idx % 2]
            o = e["o"] + lax.dot_general(e["qe"], st.astype(BF16), _NT, preferred_element_type=F32)
            e["out_ref"][e["sl"], :] = o
            states[idx % 2] = st * e["decay"] + e["upd"]
        for st, (_, st_ref, *_) in zip(states, scans):
            st_ref[...] = st
        return carry

    lax.fori_loop(0, ng, scan, 0)

    rows = _row_tile(seq, 512)

    def finish(i, carry):
        sl = pl.ds(pl.multiple_of(i * rows, rows), rows)
        o = of_ref[sl, :] + ob_ref[sl, :]
        o = o * lax.rsqrt(jnp.mean(o * o, axis=-1, keepdims=True) + EPS)
        o_ref[0, 0, sl, :] = (o * gn_ref[...] * _silu(g_ref[0, 0, sl, :].astype(F32))).astype(o_ref.dtype)
        return carry

    lax.fori_loop(0, seq // rows, finish, 0)


def hgrn_scan(qvg, gates, lb, gnorm):
    b, _, l, _ = qvg.shape
    part = lambda p: pl.BlockSpec((1, 1, l, HG_DK), lambda i, h, p=p: (i, p * HG_HEADS + h, 0, 0))
    return pl.pallas_call(
        functools.partial(_hgrn_kernel, seq=l),
        out_shape=jax.ShapeDtypeStruct((b, HG_HEADS, l, HG_DK), BF16),
        grid=(b, HG_HEADS),
        in_specs=[part(0), part(0), part(1), part(1), part(2),
                  pl.BlockSpec((1, 1, HG_DK), lambda i, h: (h, 0, 0)),
                  pl.BlockSpec((1, HG_DK), lambda i, h: (0, 0))],
        out_specs=pl.BlockSpec((1, 1, l, HG_DK), lambda i, h: (i, h, 0, 0)),
        scratch_shapes=[pltpu.VMEM((l, HG_DK), F32), pltpu.VMEM((l, HG_DK), F32),
                        pltpu.VMEM((HG_DK, HG_DK), F32), pltpu.VMEM((HG_DK, HG_DK), F32)],
        compiler_params=_params("parallel", "parallel"),
        name="hgrn_scan",
    )(qvg, gates, gates, qvg, qvg, lb, gnorm)


def _band_kernel(slope_ref, q_ref, k_ref, v_ref, o_ref, l_ref, bias_ref, *, n, bq, win, dil, pairs):
    nq = n // bq
    lane = lax.broadcasted_iota(jnp.int32, (1, LANES), 1)
    head0 = lane < ATT_HD
    qsel = (jnp.where(head0, ATT_HD ** -0.5, 0.0).astype(BF16), jnp.where(head0, 0.0, ATT_HD ** -0.5).astype(BF16))
    ones0 = jnp.broadcast_to(jnp.where(head0, 1.0, 0.0).astype(BF16), (win, LANES))
    ones1 = jnp.broadcast_to(jnp.where(head0, 0.0, 1.0).astype(BF16), (win, LANES))
    delta = (lax.broadcasted_iota(jnp.int32, (bq, win), 1) - lax.broadcasted_iota(jnp.int32, (bq, win), 0))

    def make_bias(pi, offset):
        rel = jnp.abs(delta + offset)
        base = jnp.where(rel <= ATT_RADIUS, -(dil * rel).astype(F32), NEG)
        pair = pl.program_id(2) * pairs + pi
        return [slope_ref[2 * pair + h] * base for h in range(2)]

    def scores(pi, q0, k0, bias):
        q = q_ref[0, pi, 0, pl.ds(q0, bq), :]
        k = k_ref[0, pi, 0, pl.ds(k0, win), :]
        ps, ms = [], []
        for h in range(2):
            s = lax.dot_general(q * qsel[h], k, _NT, preferred_element_type=F32) + bias[h]
            m = jnp.max(s, axis=-1, keepdims=True)
            ps.append(jnp.exp(s - m).astype(BF16))
            ms.append(m)
        return pi, q0, k0, ps, ms

    def finish(pi, q0, k0, ps, ms):
        v = v_ref[0, pi, 0, pl.ds(k0, win), :]
        rhs = jnp.concatenate([
            jnp.concatenate([jnp.where(head0, v, 0), ones0], axis=1),
            jnp.concatenate([jnp.where(head0, 0, v), ones1], axis=1)], axis=0)
        r = jnp.dot(jnp.concatenate(ps, axis=1), rhs, preferred_element_type=F32)
        den = r[:, LANES:]
        o_ref[0, pi, 0, pl.ds(q0, bq), :] = (r[:, :LANES] / den).astype(o_ref.dtype)
        l_ref[0, pi, 0, pl.ds(q0, bq), :] = jnp.where(head0, ms[0], ms[1]) + jnp.log(den)

    def blocks(todo):
        for i in range(0, len(todo), ATT_PHASE):
            for part in [scores(*t) for t in todo[i:i + ATT_PHASE]]:
                finish(*part)

    blocks([(pi, 0, 0, make_bias(pi, 0)) for pi in range(pairs)])
    if nq > 2:
        for pi in range(pairs):
            b0, b1 = make_bias(pi, -ATT_RADIUS)
            bias_ref[pi, 0] = b0
            bias_ref[pi, 1] = b1
        unroll = max(1, ATT_INFLIGHT // pairs)
        trips, rest = divmod(nq - 2, unroll)

        def interior(first, count):
            todo = []
            for u in range(count):
                q0 = (first + u) * bq
                k0 = q0 - ATT_RADIUS
                if not isinstance(first, int):
                    q0, k0 = pl.multiple_of(q0, bq), pl.multiple_of(k0, ATT_RADIUS)
                todo += [(pi, q0, k0, (bias_ref[pi, 0], bias_ref[pi, 1])) for pi in range(pairs)]
            blocks(todo)

        def body(i, carry):
            interior(1 + i * unroll, unroll)
            return carry

        lax.fori_loop(0, trips, body, 0)
        interior(1 + trips * unroll, rest)
    if nq > 1:
        blocks([(pi, n - bq, n - win, make_bias(pi, bq - win)) for pi in range(pairs)])


def band_attention(slopes, qkv):
    b, _, dil, n, _ = qkv.shape
    bq = min(ATT_QBLOCK, n)
    win = min(bq + 2 * ATT_RADIUS, n)
    pairs = max(1, min(N_SLABS, ATT_ROWS_PER_STEP // n))
    part = lambda c: pl.BlockSpec((1, pairs, 1, n, LANES), lambda bi, r, j, c=c: (bi, c * (N_SLABS // pairs) + j, r, 0, 0))
    out = pl.BlockSpec((1, pairs, 1, n, LANES), lambda bi, r, j: (bi, j, r, 0, 0))
    return pl.pallas_call(
        functools.partial(_band_kernel, n=n, bq=bq, win=win, dil=dil, pairs=pairs),
        out_shape=(jax.ShapeDtypeStruct((b, N_SLABS, dil, n, LANES), BF16),
                   jax.ShapeDtypeStruct((b, N_SLABS, dil, n, LANES), F32)),
        grid=(b, dil, N_SLABS // pairs),
        in_specs=[pl.BlockSpec(memory_space=pltpu.SMEM), part(0), part(1), part(2)],
        out_specs=(out, out),
        scratch_shapes=[pltpu.VMEM((pairs, 2, bq, win), F32)],
        compiler_params=_params("parallel", "parallel", "parallel"),
        name="band_attention",
    )(slopes, qkv, qkv, qkv)


def _trunk(x, mem, p):
    b, l, d = x.shape
    t = b * l
    kv = kv_projection(mem, p["gains"][:, 5], p["xa_w_kv"])
    for i in range(DEPTH):
        g = p["gains"][i]
        xattn = (kv, i, p["xa_w_q"][i], p["xa_w_o"][i], g[4], g[6])
        x = ffn_block(x.reshape(t, d), g[0], g[1], p["ffn_w_in"][i, 0], p["ffn_w_out"][i, 0]).reshape(b, l, d)
        j = i // 2
        if i % 2 == 0:
            nt = D_MODEL // MM_TILE_N
            qvg, gates = rms_matmul(x, g[2], p["hg_w_in"][j], [(3 * nt, BF16), (2 * nt, F32)], "planes")
            mix = [hgrn_scan(qvg.reshape(b, -1, l, LANES), gates.reshape(b, -1, l, LANES),
                             p["lower_bounds"][i], p["hg_gnorm"][j])]
            x = mixer_xattn_block(x, mix, (p["hg_w_out"][j], g[3]), xattn)
        else:
            nt = 3 * D_MODEL // MM_TILE_N
            res = [band_attention(p["slopes"], *rms_matmul(x, g[2], p["att_w_in"][j, gi], [(nt, BF16)], "planes", dil))
                   for gi, (_, dil) in enumerate(DIL_PATTERNS)]
            mix = [o for o, _ in res] + [lse for _, lse in res]
            x = mixer_xattn_block(x, mix, (p["att_w_out"][j], g[3]))
            x = mixer_xattn_block(x, xattn_params=xattn)
        x = ffn_block(x.reshape(t, d), g[7], g[8], p["ffn_w_in"][i, 1], p["ffn_w_out"][i, 1]).reshape(b, l, d)
    return x


def kernel(x_prompt, x_sample, mem_prompt, mem_sample, norm_gains, ffn_w_in, ffn_w_out, hg_w_in, hg_lb_logits, hg_gnorm, hg_w_out, att_w_in, att_w_out, xa_w_q, xa_w_kv, xa_w_o):
    sm = jax.nn.softmax(hg_lb_logits.astype(F32), axis=0)
    lower_bounds = jnp.maximum(jnp.cumsum(sm, axis=0) - sm[0], 0.0)
    p = {
        "gains": norm_gains.astype(F32).reshape(DEPTH, -1, 1, D_MODEL),
        "ffn_w_in": col_tiles(ffn_w_in, FF_CHUNK),
        "ffn_w_out": ffn_w_out.astype(BF16).reshape(DEPTH, 2, D_FF // FF_CHUNK, FF_CHUNK, D_MODEL),
        "hg_w_in": col_tiles(jnp.concatenate([hg_w_in[..., s * D_MODEL:(s + 1) * D_MODEL] for s in (0, 3, 4, 1, 2)],
                                             axis=-1), MM_TILE_N),
        "lower_bounds": lower_bounds.reshape(DEPTH, HG_HEADS, 1, HG_DK),
        "hg_gnorm": hg_gnorm.astype(F32).reshape(-1, 1, HG_DK),
        "hg_w_out": hg_w_out.astype(BF16),
        "att_w_in": col_tiles(att_w_in.reshape(-1, D_MODEL, len(DIL_PATTERNS), 3 * D_MODEL).swapaxes(1, 2), MM_TILE_N),
        "att_w_out": att_w_out.astype(BF16),
        "xa_w_q": xa_w_q.astype(BF16),
        "xa_w_kv": col_tiles(xa_w_kv, MM_TILE_N),
        "xa_w_o": xa_w_o.astype(BF16),
        "slopes": jnp.exp2(-8.0 * jnp.arange(1, ATT_HEADS + 1, dtype=F32) / ATT_HEADS),
    }
    return (_trunk(x_prompt, mem_prompt, p), _trunk(x_sample, mem_sample, p))
```

```python
import functools

import jax
import jax.numpy as jnp
from jax import lax
from jax.experimental import pallas as pl
from jax.experimental.pallas import tpu as pltpu

F32 = jnp.float32
BF16 = jnp.bfloat16

D_MODEL = 1024
LANES = 128
N_SLABS = D_MODEL // LANES
DEPTH = 4
HG_HEADS = 8
HG_DK = 128
DIL_PATTERNS = ((128, 1), (512, 4), (2048, 16))
ATT_HEADS = 16
ATT_HD = 64
ATT_RADIUS = 64
X_HEADS = 4
X_HD = 256
D_FF = 2816
EPS = 1e-6
NEG = -1e30
LOG2E = 1.4426950408889634

HG_CHUNK = 64
HG_GROUP = 16
FF_CHUNK = 256
MM_TILE_N = 1024
SUB_STRIDE = 4
ATT_QBLOCK = 128
ATT_INFLIGHT = 16
ATT_PHASE = 4
ATT_ROWS_PER_STEP = 4096
VMEM_LIMIT = 56 * 1024 * 1024

_NT = (((1,), (1,)), ((), ()))
_TN = (((0,), (0,)), ((), ()))


def _params(*sem):
    return pltpu.CompilerParams(dimension_semantics=sem, vmem_limit_bytes=VMEM_LIMIT)


def _rms(x, g):
    return x * lax.rsqrt(jnp.mean(x * x, axis=-1, keepdims=True) + EPS) * g


def _silu(x):
    return x / (1.0 + jnp.exp(-x))


def _row_tile(t, pref):
    while t % pref:
        pref //= 2
    return pref


def _slabs(ref):
    return jnp.concatenate([ref[0, c] for c in range(ref.shape[1])], axis=-1)


def _rms_matmul_kernel(x_ref, g_ref, w_ref, *refs, layout, dil, splits):
    o_refs, scratch = refs[:len(splits)], refs[len(splits):]
    h = _rms(x_ref[0], g_ref[...]).astype(BF16)
    tn = w_ref.shape[2]
    spt = tn // LANES
    for o_ref, (t0, nt) in zip(o_refs, splits):
        for j in range(nt):
            y = jnp.dot(h, w_ref[t0 + j], preferred_element_type=F32)
            if layout == "rows":
                o_ref[0, :, j * tn:(j + 1) * tn] = y.astype(o_ref.dtype)
            elif dil == 1:
                for c in range(spt):
                    o_ref[0, j * spt + c, 0] = y[:, c * LANES:(c + 1) * LANES].astype(o_ref.dtype)
            else:
                y_ref, *mid = scratch
                rows = y.shape[0] // dil
                d1 = SUB_STRIDE if dil > SUB_STRIDE else 1
                d2 = dil // d1
                for c in range(spt):
                    y_ref[c] = y[:, c * LANES:(c + 1) * LANES]
                    for q in range(d1):
                        if d1 == 1:
                            src = y_ref.at[c]
                        else:
                            mid[0][c, q] = y_ref[c, pl.ds(q, rows * d2, stride=d1), :]
                            src = mid[0].at[c, q]
                        for p in range(d2):
                            o_ref[0, j * spt + c, q + d1 * p] = src[pl.ds(p, rows, stride=d2), :].astype(o_ref.dtype)


def col_tiles(w, tn):
    *lead, d, n = w.shape
    return jnp.swapaxes(w.astype(BF16).reshape(*lead, d, n // tn, tn), -3, -2)


def rms_matmul(x, g, w_tiles, out_dtypes, layout, dil=1, tm=512):
    b, l, d = x.shape
    _, _, tn = w_tiles.shape
    tm = _row_tile(l, tm)
    out_shapes, out_specs, splits, t0 = [], [], [], 0
    for nt, dtype in out_dtypes:
        n = nt * tn
        if layout == "rows":
            out_shapes.append(jax.ShapeDtypeStruct((b, l, n), dtype))
            out_specs.append(pl.BlockSpec((1, tm, n), lambda bi, i: (bi, i, 0)))
        else:
            out_shapes.append(jax.ShapeDtypeStruct((b, n // LANES, dil, l // dil, LANES), dtype))
            out_specs.append(pl.BlockSpec((1, n // LANES, dil, tm // dil, LANES), lambda bi, i: (bi, 0, 0, i, 0)))
        splits.append((t0, nt))
        t0 += nt
    scratch = []
    if dil > 1:
        scratch.append(pltpu.VMEM((tn // LANES, tm, LANES), F32))
    if dil > SUB_STRIDE:
        scratch.append(pltpu.VMEM((tn // LANES, SUB_STRIDE, tm // SUB_STRIDE, LANES), F32))
    return pl.pallas_call(
        functools.partial(_rms_matmul_kernel, layout=layout, dil=dil, splits=tuple(splits)),
        out_shape=tuple(out_shapes),
        grid=(b, l // tm),
        in_specs=[
            pl.BlockSpec((1, tm, d), lambda bi, i: (bi, i, 0)),
            pl.BlockSpec((1, d), lambda bi, i: (0, 0)),
            pl.BlockSpec(w_tiles.shape, lambda bi, i: (0, 0, 0), pipeline_mode=pl.Buffered(1)),
        ],
        out_specs=tuple(out_specs),
        scratch_shapes=scratch,
        compiler_params=_params("parallel", "parallel"),
        name="rms_matmul",
    )(x, g, w_tiles)


def _ffn_kernel(x_ref, g0_ref, g1_ref, wi_ref, wo_ref, o_ref, acc_ref):
    nf = wo_ref.shape[0]
    half = x_ref.shape[0] // 2
    rows = (pl.ds(0, half), pl.ds(half, half))

    def norm_in(r):
        return _rms(x_ref[rows[r], :], g0_ref[...]).astype(BF16)

    def norm_out(r):
        o_ref[rows[r], :] = x_ref[rows[r], :] + 0.5 * _rms(acc_ref[rows[r], :], g1_ref[...])

    h = [norm_in(0), None]
    for r in range(2):
        for c in range(nf):
            gate = jnp.dot(h[r], wi_ref[c], preferred_element_type=F32)
            up = jnp.dot(h[r], wi_ref[nf + c], preferred_element_type=F32)
            act = (_silu(gate) * up).astype(BF16)
            down = jnp.dot(act, wo_ref[c], preferred_element_type=F32)
            if c == 0:
                acc_ref[rows[r], :] = down
            else:
                acc_ref[rows[r], :] += down
            if c == nf // 2:
                if r == 0:
                    h[1] = norm_in(1)
                else:
                    norm_out(0)
    norm_out(1)


def ffn_block(x, g0, g1, w_in, w_out, tm=1024):
    t, d = x.shape
    tm = _row_tile(t, tm)
    row = pl.BlockSpec((tm, d), lambda i: (i, 0))
    gain = pl.BlockSpec((1, d), lambda i: (0, 0))
    resident = lambda w: pl.BlockSpec(w.shape, lambda i: (0, 0, 0), pipeline_mode=pl.Buffered(1))
    return pl.pallas_call(
        _ffn_kernel,
        out_shape=jax.ShapeDtypeStruct((t, d), F32),
        grid=(t // tm,),
        in_specs=[row, gain, gain, resident(w_in), resident(w_out)],
        out_specs=row,
        scratch_shapes=[pltpu.VMEM((tm, d), F32)],
        compiler_params=_params("parallel"),
        name="ffn_block",
    )(x, g0, g1, w_in, w_out)


def _token_rows(ref, buf=None, mid=None):
    dil, sub = ref.shape[2], ref.shape[3]
    if dil == 1:
        return jnp.concatenate([ref[0, c, 0] for c in range(N_SLABS)], axis=-1).astype(F32)
    d1 = SUB_STRIDE if dil > SUB_STRIDE else 1
    d2 = dil // d1
    for c in range(N_SLABS):
        for q in range(d1):
            dst = buf.at[c] if d1 == 1 else mid.at[c, q]
            for p in range(d2):
                dst[pl.ds(p, sub, stride=d2), :] = ref[0, c, q + d1 * p].astype(F32)
            if d1 > 1:
                buf[c, pl.ds(q, sub * d2, stride=d1), :] = mid[c, q]
    return jnp.concatenate([buf[c] for c in range(N_SLABS)], axis=-1)


def _merged_groups(o0_ref, o1_ref, o2_ref, l0_ref, l1_ref, l2_ref, bl1, bl2, bo1, bo2, ml2, mo2):
    l0, l1, l2 = _token_rows(l0_ref), _token_rows(l1_ref, bl1), _token_rows(l2_ref, bl2, ml2)
    m = jnp.maximum(jnp.maximum(l0, l1), l2)
    e0, e1, e2 = jnp.exp2(l0 - m), jnp.exp2(l1 - m), jnp.exp2(l2 - m)
    num = e0 * _token_rows(o0_ref) + e1 * _token_rows(o1_ref, bo1) + e2 * _token_rows(o2_ref, bo2, mo2)
    return (num / (e0 + e1 + e2)).astype(BF16)


def _mixer_xattn_kernel(*refs, n_mix, xattn):
    mix_refs, refs = refs[:n_mix], refs[n_mix:]
    if n_mix:
        (wm_ref, gm_ref), refs = refs[:2], refs[2:]
    x_ref, refs = refs[0], refs[1:]
    if xattn:
        (k_ref, v_ref, wq_ref, wo_ref, gq_ref, go_ref), refs = refs[:6], refs[6:]
    o_ref, *scratch = refs
    x = x_ref[0]
    if n_mix:
        mixed = _slabs(mix_refs[0]) if n_mix == 1 else _merged_groups(*mix_refs, *scratch)
        x = x + _rms(jnp.dot(mixed, wm_ref[...], preferred_element_type=F32), gm_ref[...])
    if not xattn:
        o_ref[0] = x
        return
    q = jnp.dot(_rms(x, gq_ref[...]).astype(BF16), wq_ref[...], preferred_element_type=F32).astype(BF16)
    k = k_ref[0]
    v = v_ref[0]
    cols = [slice(hd * X_HD, (hd + 1) * X_HD) for hd in range(X_HEADS)]
    scores = [lax.dot_general(q[:, sl], k[:, sl], _NT, preferred_element_type=F32) * (X_HD ** -0.5) for sl in cols]
    probs = [jnp.exp(s - jnp.max(s, axis=-1, keepdims=True)) for s in scores]
    heads = [jnp.dot(p.astype(BF16), v[:, sl], preferred_element_type=F32) / jnp.sum(p, axis=-1, keepdims=True)
             for p, sl in zip(probs, cols)]
    c = jnp.dot(jnp.concatenate(heads, axis=-1).astype(BF16), wo_ref[...], preferred_element_type=F32)
    o_ref[0] = x + _rms(c, go_ref[...])


def mixer_xattn_block(x, mix=(), mix_params=(), xattn_params=()):
    b, l, d = x.shape
    merge = len(mix) > 1
    tm = _row_tile(l, 512)
    row = pl.BlockSpec((1, tm, d), lambda bi, i: (bi, i, 0))
    full = pl.BlockSpec((d, d), lambda bi, i: (0, 0), pipeline_mode=pl.Buffered(1))
    gain = pl.BlockSpec((1, d), lambda bi, i: (0, 0))
    specs, args, scratch = [], [], []
    if merge:
        specs += [pl.BlockSpec((1, N_SLABS, a.shape[2], tm // a.shape[2], LANES), lambda bi, i: (bi, 0, 0, i, 0))
                  for a in mix]
        scratch = ([pltpu.VMEM((N_SLABS, tm, LANES), F32)] * 4
                   + [pltpu.VMEM((N_SLABS, SUB_STRIDE, tm // SUB_STRIDE, LANES), F32)] * 2)
    elif mix:
        specs += [pl.BlockSpec((1, N_SLABS, tm, LANES), lambda bi, i: (bi, 0, i, 0))]
    if mix:
        specs += [full, gain]
        args += [*mix, *mix_params]
    specs.append(row)
    args.append(x)
    if xattn_params:
        kv, layer, w_q, w_o, g_q, g_o = xattn_params
        m = kv.shape[2]
        specs += [pl.BlockSpec((None, 1, m, d), lambda bi, i: (layer, bi, 0, 0)),
                  pl.BlockSpec((None, 1, m, d), lambda bi, i: (layer, bi, 0, 1)),
                  full, full, gain, gain]
        args += [kv, kv, w_q, w_o, g_q, g_o]
    return pl.pallas_call(
        functools.partial(_mixer_xattn_kernel, n_mix=len(mix), xattn=bool(xattn_params)),
        out_shape=jax.ShapeDtypeStruct((b, l, d), F32),
        grid=(b, l // tm),
        in_specs=specs,
        out_specs=row,
        scratch_shapes=scratch,
        compiler_params=_params("parallel", "parallel"),
        name="mixer_xattn_block",
    )(*args)


def kv_projection(mem, gains, w_tiles):
    b, m, d = mem.shape
    depth, nt, _, tn = w_tiles.shape
    return pl.pallas_call(
        functools.partial(_rms_matmul_kernel, layout="rows", dil=1, splits=((0, nt),)),
        out_shape=jax.ShapeDtypeStruct((depth, b, m, nt * tn), BF16),
        grid=(depth, b),
        in_specs=[
            pl.BlockSpec((1, m, d), lambda li, bi: (bi, 0, 0)),
            pl.BlockSpec((None, 1, d), lambda li, bi: (li, 0, 0)),
            pl.BlockSpec((None, nt, d, tn), lambda li, bi: (li, 0, 0, 0)),
        ],
        out_specs=pl.BlockSpec((None, 1, m, nt * tn), lambda li, bi: (li, bi, 0, 0)),
        compiler_params=_params("parallel", "parallel"),
        name="kv_projection",
    )(mem, gains, w_tiles)


def _hgrn_kernel(q_ref, zf_ref, zb_ref, v_ref, g_ref, lb_ref, gn_ref, o_ref, of_ref, ob_ref, stf_ref, stb_ref, *, seq):
    c_len = HG_CHUNK
    nc = seq // c_len
    half = c_len // 2
    lb = lb_ref[0]
    log_lb = jnp.log(lb)
    log_1m_lb = jnp.log1p(-lb)
    row = lax.broadcasted_iota(jnp.int32, (c_len, c_len), 0)
    col = lax.broadcasted_iota(jnp.int32, (c_len, c_len), 1)
    same_half = (row < half) == (col < half)
    lower = (col <= row) & same_half
    upper = (col >= row) & same_half
    lower_sum = (col <= row).astype(F32)
    upper_sum = (col >= row).astype(F32)

    def halves(first, second):
        return jnp.concatenate([jnp.broadcast_to(first, (half, HG_DK)), jnp.broadcast_to(second, (half, HG_DK))], axis=0)

    def gate(z):
        e = jnp.exp(-jnp.abs(z))
        log_sig = jnp.minimum(z, 0.0) - jnp.log(1.0 + e)
        t = log_1m_lb + log_sig
        logf = jnp.maximum(log_lb, t) + jnp.log(1.0 + jnp.exp(-jnp.abs(log_lb - t)))
        sig_neg = jnp.where(z >= 0.0, e, 1.0) / (1.0 + e)
        return logf, (1.0 - lb) * sig_neg

    grp = _row_tile(nc, HG_GROUP)
    ng = nc // grp
    quarter = half // 2
    scans = ((zf_ref, stf_ref, of_ref, lower_sum, lower, (quarter - 1, half - 1, half + quarter - 1, c_len - 1), False),
             (zb_ref, stb_ref, ob_ref, upper_sum, upper, (quarter, half, half + quarter, 0), True))

    stf_ref[...] = jnp.zeros_like(stf_ref)
    stb_ref[...] = jnp.zeros_like(stb_ref)

    def scan(i, carry):
        steps = []
        for z_ref, st_ref, out_ref, sum_mat, causal, ref_rows, rev in scans:
            g0 = (ng - 1 - i) if rev else i
            sls = [pl.ds(pl.multiple_of((g0 * grp + j) * c_len, c_len), c_len) for j in range(grp)]
            gates = [gate(z_ref[0, 0, sl, :]) for sl in sls]
            cums = jnp.dot(sum_mat, jnp.concatenate([lf for lf, _ in gates], axis=1),
                           precision=lax.Precision.HIGHEST, preferred_element_type=F32)
            for j in (reversed(range(grp)) if rev else range(grp)):
                steps.append(dict(sl=sls[j], k=gates[j][1], cum=cums[:, j * HG_DK:(j + 1) * HG_DK], causal=causal,
                                  ref_rows=ref_rows, rev=rev, st_ref=st_ref, out_ref=out_ref))
        order = [steps[d * grp + j] for j in range(grp) for d in range(2)]
        zeros = jnp.zeros((half, HG_DK), F32)
        for e in order:
            cum = e["cum"]
            m_a, m_mid, m_b, tot = (cum[r:r + 1, :] for r in e["ref_rows"])
            q = _silu(q_ref[0, 0, e["sl"], :].astype(F32))
            e["v"] = v_ref[0, 0, e["sl"], :]
            ref = halves(m_a, m_b)
            qd = q * jnp.exp(cum - ref)
            kd = e["k"] * jnp.exp(ref - cum)
            e["qe"] = (qd * halves(jnp.exp(m_a), jnp.exp(m_b))).astype(BF16)
            ke = (kd * halves(jnp.exp(tot - m_a), jnp.exp(tot - m_b))).astype(BF16)
            e["decay"] = jnp.exp(tot)
            if e["rev"]:
                qx = jnp.concatenate([qd[:half] * jnp.exp(m_a - m_mid), zeros], axis=0)
                kx = jnp.concatenate([zeros, kd[half:] * jnp.exp(m_mid - m_b)], axis=0)
            else:
                qx = jnp.concatenate([zeros, qd[half:] * jnp.exp(m_b - m_mid)], axis=0)
                kx = jnp.concatenate([kd[:half] * jnp.exp(m_mid - m_a), zeros], axis=0)
            e["s_in"] = lax.dot_general(qd.astype(BF16), kd.astype(BF16), _NT, preferred_element_type=F32)
            e["s_x"] = lax.dot_general(qx.astype(BF16), kx.astype(BF16), _NT, preferred_element_type=F32)
            e["upd"] = lax.dot_general(e["v"], ke, _TN, preferred_element_type=F32)
        for e in order:
            s = (jnp.where(e["causal"], e["s_in"], 0.0) + e["s_x"]).astype(BF16)
            e["o"] = jnp.dot(s, e["v"], preferred_element_type=F32)
        states = [st_ref[...] for _, st_ref, *_ in scans]
        for idx, e in enumerate(order):
            st = states[idx % 2]
            o = e["o"] + lax.dot_general(e["qe"], st.astype(BF16), _NT, preferred_element_type=F32)
            e["out_ref"][e["sl"], :] = o
            states[idx % 2] = st * e["decay"] + e["upd"]
        for st, (_, st_ref, *_) in zip(states, scans):
            st_ref[...] = st
        return carry

    lax.fori_loop(0, ng, scan, 0)

    rows = _row_tile(seq, 512)

    def finish(i, carry):
        sl = pl.ds(pl.multiple_of(i * rows, rows), rows)
        o = of_ref[sl, :] + ob_ref[sl, :]
        o = o * lax.rsqrt(jnp.mean(o * o, axis=-1, keepdims=True) + EPS)
        o_ref[0, 0, sl, :] = (o * gn_ref[...] * _silu(g_ref[0, 0, sl, :].astype(F32))).astype(o_ref.dtype)
        return carry

    lax.fori_loop(0, seq // rows, finish, 0)


def hgrn_scan(qvg, gates, lb, gnorm):
    b, _, l, _ = qvg.shape
    part = lambda p: pl.BlockSpec((1, 1, l, HG_DK), lambda i, h, p=p: (i, p * HG_HEADS + h, 0, 0))
    return pl.pallas_call(
        functools.partial(_hgrn_kernel, seq=l),
        out_shape=jax.ShapeDtypeStruct((b, HG_HEADS, l, HG_DK), BF16),
        grid=(b, HG_HEADS),
        in_specs=[part(0), part(0), part(1), part(1), part(2),
                  pl.BlockSpec((1, 1, HG_DK), lambda i, h: (h, 0, 0)),
                  pl.BlockSpec((1, HG_DK), lambda i, h: (0, 0))],
        out_specs=pl.BlockSpec((1, 1, l, HG_DK), lambda i, h: (i, h, 0, 0)),
        scratch_shapes=[pltpu.VMEM((l, HG_DK), F32), pltpu.VMEM((l, HG_DK), F32),
                        pltpu.VMEM((HG_DK, HG_DK), F32), pltpu.VMEM((HG_DK, HG_DK), F32)],
        compiler_params=_params("parallel", "parallel"),
        name="hgrn_scan",
    )(qvg, gates, gates, qvg, qvg, lb, gnorm)


def _band_kernel(slope_ref, q_ref, k_ref, v_ref, o_ref, l_ref, bias_ref, *, n, bq, win, dil, pairs):
    nq = n // bq
    lane = lax.broadcasted_iota(jnp.int32, (1, LANES), 1)
    head0 = lane < ATT_HD
    qsel = (jnp.where(head0, 1.0, 0.0).astype(BF16), jnp.where(head0, 0.0, 1.0).astype(BF16))
    ones0 = jnp.broadcast_to(jnp.where(head0, 1.0, 0.0).astype(BF16), (win, LANES))
    ones1 = jnp.broadcast_to(jnp.where(head0, 0.0, 1.0).astype(BF16), (win, LANES))
    delta = (lax.broadcasted_iota(jnp.int32, (bq, win), 1) - lax.broadcasted_iota(jnp.int32, (bq, win), 0))

    def make_bias(pi, offset):
        rel = jnp.abs(delta + offset)
        base = jnp.where(rel <= ATT_RADIUS, -(dil * rel).astype(F32), NEG)
        pair = pl.program_id(2) * pairs + pi
        return [slope_ref[2 * pair + h] * base for h in range(2)]

    def scores(pi, q0, k0, bias):
        q = q_ref[0, pi, 0, pl.ds(q0, bq), :]
        k = k_ref[0, pi, 0, pl.ds(k0, win), :]
        ps, ms = [], []
        for h in range(2):
            s = lax.dot_general(q * qsel[h], k, _NT, preferred_element_type=F32) + bias[h]
            m = jnp.max(s, axis=-1, keepdims=True)
            ps.append(jnp.exp2(s - m).astype(BF16))
            ms.append(m)
        return pi, q0, k0, ps, ms

    def finish(pi, q0, k0, ps, ms):
        v = v_ref[0, pi, 0, pl.ds(k0, win), :]
        rhs = jnp.concatenate([
            jnp.concatenate([jnp.where(head0, v, 0), ones0], axis=1),
            jnp.concatenate([jnp.where(head0, 0, v), ones1], axis=1)], axis=0)
        r = jnp.dot(jnp.concatenate(ps, axis=1), rhs, preferred_element_type=F32)
        den = r[:, LANES:]
        o_ref[0, pi, 0, pl.ds(q0, bq), :] = (r[:, :LANES] / den).astype(o_ref.dtype)
        l_ref[0, pi, 0, pl.ds(q0, bq), :] = jnp.where(head0, ms[0], ms[1]) + jnp.log2(den)

    def blocks(todo):
        for i in range(0, len(todo), ATT_PHASE):
            for part in [scores(*t) for t in todo[i:i + ATT_PHASE]]:
                finish(*part)

    blocks([(pi, 0, 0, make_bias(pi, 0)) for pi in range(pairs)])
    if nq > 2:
        for pi in range(pairs):
            b0, b1 = make_bias(pi, -ATT_RADIUS)
            bias_ref[pi, 0] = b0
            bias_ref[pi, 1] = b1
        unroll = max(1, ATT_INFLIGHT // pairs)
        trips, rest = divmod(nq - 2, unroll)

        def interior(first, count):
            todo = []
            for u in range(count):
                q0 = (first + u) * bq
                k0 = q0 - ATT_RADIUS
                if not isinstance(first, int):
                    q0, k0 = pl.multiple_of(q0, bq), pl.multiple_of(k0, ATT_RADIUS)
                todo += [(pi, q0, k0, (bias_ref[pi, 0], bias_ref[pi, 1])) for pi in range(pairs)]
            blocks(todo)

        def body(i, carry):
            interior(1 + i * unroll, unroll)
            return carry

        lax.fori_loop(0, trips, body, 0)
        interior(1 + trips * unroll, rest)
    if nq > 1:
        blocks([(pi, n - bq, n - win, make_bias(pi, bq - win)) for pi in range(pairs)])


def band_attention(slopes, qkv):
    b, _, dil, n, _ = qkv.shape
    bq = min(ATT_QBLOCK, n)
    win = min(bq + 2 * ATT_RADIUS, n)
    pairs = max(1, min(N_SLABS, ATT_ROWS_PER_STEP // n))
    part = lambda c: pl.BlockSpec((1, pairs, 1, n, LANES), lambda bi, r, j, c=c: (bi, c * (N_SLABS // pairs) + j, r, 0, 0))
    out = pl.BlockSpec((1, pairs, 1, n, LANES), lambda bi, r, j: (bi, j, r, 0, 0))
    return pl.pallas_call(
        functools.partial(_band_kernel, n=n, bq=bq, win=win, dil=dil, pairs=pairs),
        out_shape=(jax.ShapeDtypeStruct((b, N_SLABS, dil, n, LANES), BF16),
                   jax.ShapeDtypeStruct((b, N_SLABS, dil, n, LANES), F32)),
        grid=(b, dil, N_SLABS // pairs),
        in_specs=[pl.BlockSpec(memory_space=pltpu.SMEM), part(0), part(1), part(2)],
        out_specs=(out, out),
        scratch_shapes=[pltpu.VMEM((pairs, 2, bq, win), F32)],
        compiler_params=_params("parallel", "parallel", "parallel"),
        name="band_attention",
    )(slopes, qkv, qkv, qkv)


def _trunk(x, mem, p):
    b, l, d = x.shape
    t = b * l
    kv = kv_projection(mem, p["gains"][:, 5], p["xa_w_kv"])
    for i in range(DEPTH):
        g = p["gains"][i]
        xattn = (kv, i, p["xa_w_q"][i], p["xa_w_o"][i], g[4], g[6])
        x = ffn_block(x.reshape(t, d), g[0], g[1], p["ffn_w_in"][i, 0], p["ffn_w_out"][i, 0]).reshape(b, l, d)
        j = i // 2
        if i % 2 == 0:
            nt = D_MODEL // MM_TILE_N
            qvg, gates = rms_matmul(x, g[2], p["hg_w_in"][j], [(3 * nt, BF16), (2 * nt, F32)], "planes")
            mix = [hgrn_scan(qvg.reshape(b, -1, l, LANES), gates.reshape(b, -1, l, LANES),
                             p["lower_bounds"][i], p["hg_gnorm"][j])]
            x = mixer_xattn_block(x, mix, (p["hg_w_out"][j], g[3]), xattn)
        else:
            nt = 3 * D_MODEL // MM_TILE_N
            res = [band_attention(p["slopes"], *rms_matmul(x, g[2], p["att_w_in"][j, gi], [(nt, BF16)], "planes", dil))
                   for gi, (_, dil) in enumerate(DIL_PATTERNS)]
            mix = [o for o, _ in res] + [lse for _, lse in res]
            x = mixer_xattn_block(x, mix, (p["att_w_out"][j], g[3]))
            x = mixer_xattn_block(x, xattn_params=xattn)
        x = ffn_block(x.reshape(t, d), g[7], g[8], p["ffn_w_in"][i, 1], p["ffn_w_out"][i, 1]).reshape(b, l, d)
    return x


def kernel(x_prompt, x_sample, mem_prompt, mem_sample, norm_gains, ffn_w_in, ffn_w_out, hg_w_in, hg_lb_logits, hg_gnorm, hg_w_out, att_w_in, att_w_out, xa_w_q, xa_w_kv, xa_w_o):
    sm = jax.nn.softmax(hg_lb_logits.astype(F32), axis=0)
    lower_bounds = jnp.maximum(jnp.cumsum(sm, axis=0) - sm[0], 0.0)
    p = {
        "gains": norm_gains.astype(F32).reshape(DEPTH, -1, 1, D_MODEL),
        "ffn_w_in": col_tiles(ffn_w_in, FF_CHUNK),
        "ffn_w_out": ffn_w_out.astype(BF16).reshape(DEPTH, 2, D_FF // FF_CHUNK, FF_CHUNK, D_MODEL),
        "hg_w_in": col_tiles(jnp.concatenate([hg_w_in[..., s * D_MODEL:(s + 1) * D_MODEL] for s in (0, 3, 4, 1, 2)],
                                             axis=-1), MM_TILE_N),
        "lower_bounds": lower_bounds.reshape(DEPTH, HG_HEADS, 1, HG_DK),
        "hg_gnorm": hg_gnorm.astype(F32).reshape(-1, 1, HG_DK),
        "hg_w_out": hg_w_out.astype(BF16),
        "att_w_in": col_tiles((att_w_in.reshape(-1, D_MODEL, len(DIL_PATTERNS), 3, D_MODEL)
                               * jnp.array([ATT_HD ** -0.5 * LOG2E, 1.0, 1.0], F32)[:, None])
                              .reshape(-1, D_MODEL, len(DIL_PATTERNS), 3 * D_MODEL).swapaxes(1, 2), MM_TILE_N),
        "att_w_out": att_w_out.astype(BF16),
        "xa_w_q": xa_w_q.astype(BF16),
        "xa_w_kv": col_tiles(xa_w_kv, MM_TILE_N),
        "xa_w_o": xa_w_o.astype(BF16),
        "slopes": LOG2E * jnp.exp2(-8.0 * jnp.arange(1, ATT_HEADS + 1, dtype=F32) / ATT_HEADS),
    }
    return (_trunk(x_prompt, mem_prompt, p), _trunk(x_sample, mem_sample, p))
```

```python
import functools

import jax
import jax.numpy as jnp
from jax import lax
from jax.experimental import pallas as pl
from jax.experimental.pallas import tpu as pltpu

F32 = jnp.float32
BF16 = jnp.bfloat16

D_MODEL = 1024
LANES = 128
N_SLABS = D_MODEL // LANES
DEPTH = 4
HG_HEADS = 8
HG_DK = 128
DIL_PATTERNS = ((128, 1), (512, 4), (2048, 16))
ATT_HEADS = 16
ATT_HD = 64
ATT_RADIUS = 64
X_HEADS = 4
X_HD = 256
D_FF = 2816
EPS = 1e-6
NEG = -1e30
LOG2E = 1.4426950408889634

HG_CHUNK = 64
HG_GROUP = 16
FF_CHUNK = 256
MM_TILE_N = 1024
SUB_STRIDE = 4
ATT_QBLOCK = 128
ATT_INFLIGHT = 16
ATT_PHASE = 4
ATT_ROWS_PER_STEP = 4096
VMEM_LIMIT = 56 * 1024 * 1024

_NT = (((1,), (1,)), ((), ()))
_TN = (((0,), (0,)), ((), ()))


def _params(*sem):
    return pltpu.CompilerParams(dimension_semantics=sem, vmem_limit_bytes=VMEM_LIMIT)


def _rms(x, g):
    return x * lax.rsqrt(jnp.mean(x * x, axis=-1, keepdims=True) + EPS) * g


def _silu(x):
    return x / (1.0 + jnp.exp(-x))


def _row_tile(t, pref):
    while t % pref:
        pref //= 2
    return pref


def _slabs(ref):
    return jnp.concatenate([ref[0, c] for c in range(ref.shape[1])], axis=-1)


def _rms_matmul_kernel(x_ref, g_ref, w_ref, *refs, layout, dil, splits):
    o_refs, scratch = refs[:len(splits)], refs[len(splits):]
    h = _rms(x_ref[0], g_ref[...]).astype(BF16)
    tn = MM_TILE_N
    spt = tn // LANES
    for o_ref, starts in zip(o_refs, splits):
        for j, c0 in enumerate(starts):
            y = jnp.dot(h, w_ref[:, c0:c0 + tn], preferred_element_type=F32)
            if layout == "rows":
                o_ref[0, :, j * tn:(j + 1) * tn] = y.astype(o_ref.dtype)
            elif dil == 1:
                for c in range(spt):
                    o_ref[0, j * spt + c, 0] = y[:, c * LANES:(c + 1) * LANES].astype(o_ref.dtype)
            else:
                y_ref, *mid = scratch
                rows = y.shape[0] // dil
                d1 = SUB_STRIDE if dil > SUB_STRIDE else 1
                d2 = dil // d1
                for c in range(spt):
                    y_ref[c] = y[:, c * LANES:(c + 1) * LANES]
                    for q in range(d1):
                        if d1 == 1:
                            src = y_ref.at[c]
                        else:
                            mid[0][c, q] = y_ref[c, pl.ds(q, rows * d2, stride=d1), :]
                            src = mid[0].at[c, q]
                        for p in range(d2):
                            o_ref[0, j * spt + c, q + d1 * p] = src[pl.ds(p, rows, stride=d2), :].astype(o_ref.dtype)


def _stacked(w, index):
    lead = len(index)
    return pl.BlockSpec((None,) * lead + w.shape[lead:], lambda *_: (*index, 0, 0), pipeline_mode=pl.Buffered(1))


def rms_matmul(x, g, w, w_index, outputs, layout, dil=1, tm=512):
    b, l, d = x.shape
    tn = MM_TILE_N
    tm = _row_tile(l, tm)
    out_shapes, out_specs, splits = [], [], []
    for ranges, dtype in outputs:
        n = sum(cnt for _, cnt in ranges)
        if layout == "rows":
            out_shapes.append(jax.ShapeDtypeStruct((b, l, n), dtype))
            out_specs.append(pl.BlockSpec((1, tm, n), lambda bi, i: (bi, i, 0)))
        else:
            out_shapes.append(jax.ShapeDtypeStruct((b, n // LANES, dil, l // dil, LANES), dtype))
            out_specs.append(pl.BlockSpec((1, n // LANES, dil, tm // dil, LANES), lambda bi, i: (bi, 0, 0, i, 0)))
        splits.append(tuple(c for c0, cnt in ranges for c in range(c0, c0 + cnt, tn)))
    scratch = []
    if dil > 1:
        scratch.append(pltpu.VMEM((tn // LANES, tm, LANES), F32))
    if dil > SUB_STRIDE:
        scratch.append(pltpu.VMEM((tn // LANES, SUB_STRIDE, tm // SUB_STRIDE, LANES), F32))
    return pl.pallas_call(
        functools.partial(_rms_matmul_kernel, layout=layout, dil=dil, splits=tuple(splits)),
        out_shape=tuple(out_shapes),
        grid=(b, l // tm),
        in_specs=[
            pl.BlockSpec((1, tm, d), lambda bi, i: (bi, i, 0)),
            pl.BlockSpec((1, d), lambda bi, i: (0, 0)),
            _stacked(w, w_index),
        ],
        out_specs=tuple(out_specs),
        scratch_shapes=scratch,
        compiler_params=_params("parallel", "parallel"),
        name="rms_matmul",
    )(x, g, w)


def _ffn_kernel(x_ref, g0_ref, g1_ref, wi_ref, wo_ref, o_ref, acc_ref):
    nf = D_FF // FF_CHUNK
    cols = [slice(c * FF_CHUNK, (c + 1) * FF_CHUNK) for c in range(2 * nf)]
    half = x_ref.shape[0] // 2
    rows = (pl.ds(0, half), pl.ds(half, half))

    def norm_in(r):
        return _rms(x_ref[rows[r], :], g0_ref[...]).astype(BF16)

    def norm_out(r):
        o_ref[rows[r], :] = x_ref[rows[r], :] + 0.5 * _rms(acc_ref[rows[r], :], g1_ref[...])

    h = [norm_in(0), None]
    for r in range(2):
        for c in range(nf):
            gate = jnp.dot(h[r], wi_ref[:, cols[c]], preferred_element_type=F32)
            up = jnp.dot(h[r], wi_ref[:, cols[nf + c]], preferred_element_type=F32)
            act = (_silu(gate) * up).astype(BF16)
            down = jnp.dot(act, wo_ref[cols[c], :], preferred_element_type=F32)
            if c == 0:
                acc_ref[rows[r], :] = down
            else:
                acc_ref[rows[r], :] += down
            if c == nf // 2:
                if r == 0:
                    h[1] = norm_in(1)
                else:
                    norm_out(0)
    norm_out(1)


def ffn_block(x, g0, g1, w_in, w_out, w_index, tm=1024):
    t, d = x.shape
    tm = _row_tile(t, tm)
    row = pl.BlockSpec((tm, d), lambda i: (i, 0))
    gain = pl.BlockSpec((1, d), lambda i: (0, 0))
    return pl.pallas_call(
        _ffn_kernel,
        out_shape=jax.ShapeDtypeStruct((t, d), F32),
        grid=(t // tm,),
        in_specs=[row, gain, gain, _stacked(w_in, w_index), _stacked(w_out, w_index)],
        out_specs=row,
        scratch_shapes=[pltpu.VMEM((tm, d), F32)],
        compiler_params=_params("parallel"),
        name="ffn_block",
    )(x, g0, g1, w_in, w_out)


def _token_rows(ref, buf=None, mid=None):
    dil, sub = ref.shape[2], ref.shape[3]
    if dil == 1:
        return jnp.concatenate([ref[0, c, 0] for c in range(N_SLABS)], axis=-1).astype(F32)
    d1 = SUB_STRIDE if dil > SUB_STRIDE else 1
    d2 = dil // d1
    for c in range(N_SLABS):
        for q in range(d1):
            dst = buf.at[c] if d1 == 1 else mid.at[c, q]
            for p in range(d2):
                dst[pl.ds(p, sub, stride=d2), :] = ref[0, c, q + d1 * p].astype(F32)
            if d1 > 1:
                buf[c, pl.ds(q, sub * d2, stride=d1), :] = mid[c, q]
    return jnp.concatenate([buf[c] for c in range(N_SLABS)], axis=-1)


def _merged_groups(o0_ref, o1_ref, o2_ref, l0_ref, l1_ref, l2_ref, bl1, bl2, bo1, bo2, ml2, mo2):
    l0, l1, l2 = _token_rows(l0_ref), _token_rows(l1_ref, bl1), _token_rows(l2_ref, bl2, ml2)
    m = jnp.maximum(jnp.maximum(l0, l1), l2)
    e0, e1, e2 = jnp.exp2(l0 - m), jnp.exp2(l1 - m), jnp.exp2(l2 - m)
    num = e0 * _token_rows(o0_ref) + e1 * _token_rows(o1_ref, bo1) + e2 * _token_rows(o2_ref, bo2, mo2)
    return (num / (e0 + e1 + e2)).astype(BF16)


def _mixer_xattn_kernel(*refs, n_mix, xattn):
    mix_refs, refs = refs[:n_mix], refs[n_mix:]
    if n_mix:
        (wm_ref, gm_ref), refs = refs[:2], refs[2:]
    x_ref, refs = refs[0], refs[1:]
    if xattn:
        (k_ref, v_ref, wq_ref, wo_ref, gq_ref, go_ref), refs = refs[:6], refs[6:]
    o_ref, *scratch = refs
    x = x_ref[0]
    if n_mix:
        mixed = _slabs(mix_refs[0]) if n_mix == 1 else _merged_groups(*mix_refs, *scratch)
        x = x + _rms(jnp.dot(mixed, wm_ref[...], preferred_element_type=F32), gm_ref[...])
    if not xattn:
        o_ref[0] = x
        return
    q = jnp.dot(_rms(x, gq_ref[...]).astype(BF16), wq_ref[...], preferred_element_type=F32).astype(BF16)
    k = k_ref[0]
    v = v_ref[0]
    cols = [slice(hd * X_HD, (hd + 1) * X_HD) for hd in range(X_HEADS)]
    scores = [lax.dot_general(q[:, sl], k[:, sl], _NT, preferred_element_type=F32) * (X_HD ** -0.5) for sl in cols]
    probs = [jnp.exp(s - jnp.max(s, axis=-1, keepdims=True)) for s in scores]
    heads = [jnp.dot(p.astype(BF16), v[:, sl], preferred_element_type=F32) / jnp.sum(p, axis=-1, keepdims=True)
             for p, sl in zip(probs, cols)]
    c = jnp.dot(jnp.concatenate(heads, axis=-1).astype(BF16), wo_ref[...], preferred_element_type=F32)
    o_ref[0] = x + _rms(c, go_ref[...])


def mixer_xattn_block(x, mix=(), mix_params=(), xattn_params=()):
    b, l, d = x.shape
    merge = len(mix) > 1
    tm = _row_tile(l, 512)
    row = pl.BlockSpec((1, tm, d), lambda bi, i: (bi, i, 0))
    full = pl.BlockSpec((d, d), lambda bi, i: (0, 0), pipeline_mode=pl.Buffered(1))
    gain = pl.BlockSpec((1, d), lambda bi, i: (0, 0))
    specs, args, scratch = [], [], []
    if merge:
        specs += [pl.BlockSpec((1, N_SLABS, a.shape[2], tm // a.shape[2], LANES), lambda bi, i: (bi, 0, 0, i, 0))
                  for a in mix]
        scratch = ([pltpu.VMEM((N_SLABS, tm, LANES), F32)] * 4
                   + [pltpu.VMEM((N_SLABS, SUB_STRIDE, tm // SUB_STRIDE, LANES), F32)] * 2)
    elif mix:
        specs += [pl.BlockSpec((1, N_SLABS, tm, LANES), lambda bi, i: (bi, 0, i, 0))]
    if mix:
        specs += [full, gain]
        args += [*mix, *mix_params]
    specs.append(row)
    args.append(x)
    if xattn_params:
        kv, layer, w_q, w_o, g_q, g_o = xattn_params
        m = kv.shape[2]
        specs += [pl.BlockSpec((None, 1, m, d), lambda bi, i: (layer, bi, 0, 0)),
                  pl.BlockSpec((None, 1, m, d), lambda bi, i: (layer, bi, 0, 1)),
                  full, full, gain, gain]
        args += [kv, kv, w_q, w_o, g_q, g_o]
    return pl.pallas_call(
        functools.partial(_mixer_xattn_kernel, n_mix=len(mix), xattn=bool(xattn_params)),
        out_shape=jax.ShapeDtypeStruct((b, l, d), F32),
        grid=(b, l // tm),
        in_specs=specs,
        out_specs=row,
        scratch_shapes=scratch,
        compiler_params=_params("parallel", "parallel"),
        name="mixer_xattn_block",
    )(*args)


def kv_projection(mem, gains, w):
    b, m, d = mem.shape
    depth, _, n = w.shape
    return pl.pallas_call(
        functools.partial(_rms_matmul_kernel, layout="rows", dil=1, splits=(tuple(range(0, n, MM_TILE_N)),)),
        out_shape=jax.ShapeDtypeStruct((depth, b, m, n), BF16),
        grid=(depth, b),
        in_specs=[
            pl.BlockSpec((1, m, d), lambda li, bi: (bi, 0, 0)),
            pl.BlockSpec((None, 1, d), lambda li, bi: (li, 0, 0)),
            pl.BlockSpec((None, d, n), lambda li, bi: (li, 0, 0)),
        ],
        out_specs=pl.BlockSpec((None, 1, m, n), lambda li, bi: (li, bi, 0, 0)),
        compiler_params=_params("parallel", "parallel"),
        name="kv_projection",
    )(mem, gains, w)


def _hgrn_kernel(q_ref, zf_ref, zb_ref, v_ref, g_ref, lb_ref, gn_ref, o_ref, of_ref, ob_ref, stf_ref, stb_ref, *, seq):
    c_len = HG_CHUNK
    nc = seq // c_len
    half = c_len // 2
    lb = lb_ref[0]
    log_lb = jnp.log(lb)
    log_1m_lb = jnp.log1p(-lb)
    row = lax.broadcasted_iota(jnp.int32, (c_len, c_len), 0)
    col = lax.broadcasted_iota(jnp.int32, (c_len, c_len), 1)
    same_half = (row < half) == (col < half)
    lower = (col <= row) & same_half
    upper = (col >= row) & same_half
    lower_sum = (col <= row).astype(F32)
    upper_sum = (col >= row).astype(F32)

    def halves(first, second):
        return jnp.concatenate([jnp.broadcast_to(first, (half, HG_DK)), jnp.broadcast_to(second, (half, HG_DK))], axis=0)

    def gate(z):
        e = jnp.exp(-jnp.abs(z))
        log_sig = jnp.minimum(z, 0.0) - jnp.log(1.0 + e)
        t = log_1m_lb + log_sig
        logf = jnp.maximum(log_lb, t) + jnp.log(1.0 + jnp.exp(-jnp.abs(log_lb - t)))
        sig_neg = jnp.where(z >= 0.0, e, 1.0) / (1.0 + e)
        return logf, (1.0 - lb) * sig_neg

    grp = _row_tile(nc, HG_GROUP)
    ng = nc // grp
    quarter = half // 2
    scans = ((zf_ref, stf_ref, of_ref, lower_sum, lower, (quarter - 1, half - 1, half + quarter - 1, c_len - 1), False),
             (zb_ref, stb_ref, ob_ref, upper_sum, upper, (quarter, half, half + quarter, 0), True))

    stf_ref[...] = jnp.zeros_like(stf_ref)
    stb_ref[...] = jnp.zeros_like(stb_ref)

    def scan(i, carry):
        steps = []
        for z_ref, st_ref, out_ref, sum_mat, causal, ref_rows, rev in scans:
            g0 = (ng - 1 - i) if rev else i
            sls = [pl.ds(pl.multiple_of((g0 * grp + j) * c_len, c_len), c_len) for j in range(grp)]
            gates = [gate(z_ref[0, 0, sl, :]) for sl in sls]
            cums = jnp.dot(sum_mat, jnp.concatenate([lf for lf, _ in gates], axis=1),
                           precision=lax.Precision.HIGHEST, preferred_element_type=F32)
            for j in (reversed(range(grp)) if rev else range(grp)):
                steps.append(dict(sl=sls[j], k=gates[j][1], cum=cums[:, j * HG_DK:(j + 1) * HG_DK], causal=causal,
                                  ref_rows=ref_rows, rev=rev, st_ref=st_ref, out_ref=out_ref))
        order = [steps[d * grp + j] for j in range(grp) for d in range(2)]
        zeros = jnp.zeros((half, HG_DK), F32)
        for e in order:
            cum = e["cum"]
            m_a, m_mid, m_b, tot = (cum[r:r + 1, :] for r in e["ref_rows"])
            q = _silu(q_ref[0, 0, e["sl"], :].astype(F32))
            e["v"] = v_ref[0, 0, e["sl"], :]
            ref = halves(m_a, m_b)
            qd = q * jnp.exp(cum - ref)
            kd = e["k"] * jnp.exp(ref - cum)
            e["qe"] = (qd * halves(jnp.exp(m_a), jnp.exp(m_b))).astype(BF16)
            ke = (kd * halves(jnp.exp(tot - m_a), jnp.exp(tot - m_b))).astype(BF16)
            e["decay"] = jnp.exp(tot)
            if e["rev"]:
                qx = jnp.concatenate([qd[:half] * jnp.exp(m_a - m_mid), zeros], axis=0)
                kx = jnp.concatenate([zeros, kd[half:] * jnp.exp(m_mid - m_b)], axis=0)
            else:
                qx = jnp.concatenate([zeros, qd[half:] * jnp.exp(m_b - m_mid)], axis=0)
                kx = jnp.concatenate([kd[:half] * jnp.exp(m_mid - m_a), zeros], axis=0)
            e["s_in"] = lax.dot_general(qd.astype(BF16), kd.astype(BF16), _NT, preferred_element_type=F32)
            e["s_x"] = lax.dot_general(qx.astype(BF16), kx.astype(BF16), _NT, preferred_element_type=F32)
            e["upd"] = lax.dot_general(e["v"], ke, _TN, preferred_element_type=F32)
        for e in order:
            s = (jnp.where(e["causal"], e["s_in"], 0.0) + e["s_x"]).astype(BF16)
            e["o"] = jnp.dot(s, e["v"], preferred_element_type=F32)
        states = [st_ref[...] for _, st_ref, *_ in scans]
        for idx, e in enumerate(order):
            st = states[idx % 2]
            o = e["o"] + lax.dot_general(e["qe"], st.astype(BF16), _NT, preferred_element_type=F32)
            e["out_ref"][e["sl"], :] = o
            states[idx % 2] = st * e["decay"] + e["upd"]
        for st, (_, st_ref, *_) in zip(states, scans):
            st_ref[...] = st
        return carry

    lax.fori_loop(0, ng, scan, 0)

    rows = _row_tile(seq, 512)

    def finish(i, carry):
        sl = pl.ds(pl.multiple_of(i * rows, rows), rows)
        o = of_ref[sl, :] + ob_ref[sl, :]
        o = o * lax.rsqrt(jnp.mean(o * o, axis=-1, keepdims=True) + EPS)
        o_ref[0, 0, sl, :] = (o * gn_ref[...] * _silu(g_ref[0, 0, sl, :].astype(F32))).astype(o_ref.dtype)
        return carry

    lax.fori_loop(0, seq // rows, finish, 0)


def hgrn_scan(qvg, gates, lb, gnorm):
    b, _, l, _ = qvg.shape
    part = lambda p: pl.BlockSpec((1, 1, l, HG_DK), lambda i, h, p=p: (i, p * HG_HEADS + h, 0, 0))
    return pl.pallas_call(
        functools.partial(_hgrn_kernel, seq=l),
        out_shape=jax.ShapeDtypeStruct((b, HG_HEADS, l, HG_DK), BF16),
        grid=(b, HG_HEADS),
        in_specs=[part(0), part(0), part(1), part(1), part(2),
                  pl.BlockSpec((1, 1, HG_DK), lambda i, h: (h, 0, 0)),
                  pl.BlockSpec((1, HG_DK), lambda i, h: (0, 0))],
        out_specs=pl.BlockSpec((1, 1, l, HG_DK), lambda i, h: (i, h, 0, 0)),
        scratch_shapes=[pltpu.VMEM((l, HG_DK), F32), pltpu.VMEM((l, HG_DK), F32),
                        pltpu.VMEM((HG_DK, HG_DK), F32), pltpu.VMEM((HG_DK, HG_DK), F32)],
        compiler_params=_params("parallel", "parallel"),
        name="hgrn_scan",
    )(qvg, gates, gates, qvg, qvg, lb, gnorm)


def _band_kernel(slope_ref, q_ref, k_ref, v_ref, o_ref, l_ref, bias_ref, *, n, bq, win, dil, pairs):
    nq = n // bq
    lane = lax.broadcasted_iota(jnp.int32, (1, LANES), 1)
    head0 = lane < ATT_HD
    qsel = (jnp.where(head0, 1.0, 0.0).astype(BF16), jnp.where(head0, 0.0, 1.0).astype(BF16))
    ones0 = jnp.broadcast_to(jnp.where(head0, 1.0, 0.0).astype(BF16), (win, LANES))
    ones1 = jnp.broadcast_to(jnp.where(head0, 0.0, 1.0).astype(BF16), (win, LANES))
    delta = (lax.broadcasted_iota(jnp.int32, (bq, win), 1) - lax.broadcasted_iota(jnp.int32, (bq, win), 0))

    def make_bias(pi, offset):
        rel = jnp.abs(delta + offset)
        base = jnp.where(rel <= ATT_RADIUS, -(dil * rel).astype(F32), NEG)
        pair = pl.program_id(2) * pairs + pi
        return [slope_ref[2 * pair + h] * base for h in range(2)]

    def scores(pi, q0, k0, bias):
        q = q_ref[0, pi, 0, pl.ds(q0, bq), :]
        k = k_ref[0, pi, 0, pl.ds(k0, win), :]
        ps, ms = [], []
        for h in range(2):
            s = lax.dot_general(q * qsel[h], k, _NT, preferred_element_type=F32) + bias[h]
            m = jnp.max(s, axis=-1, keepdims=True)
            ps.append(jnp.exp2(s - m).astype(BF16))
            ms.append(m)
        return pi, q0, k0, ps, ms

    def finish(pi, q0, k0, ps, ms):
        v = v_ref[0, pi, 0, pl.ds(k0, win), :]
        rhs = jnp.concatenate([
            jnp.concatenate([jnp.where(head0, v, 0), ones0], axis=1),
            jnp.concatenate([jnp.where(head0, 0, v), ones1], axis=1)], axis=0)
        r = jnp.dot(jnp.concatenate(ps, axis=1), rhs, preferred_element_type=F32)
        den = r[:, LANES:]
        o_ref[0, pi, 0, pl.ds(q0, bq), :] = (r[:, :LANES] / den).astype(o_ref.dtype)
        l_ref[0, pi, 0, pl.ds(q0, bq), :] = jnp.where(head0, ms[0], ms[1]) + jnp.log2(den)

    def blocks(todo):
        for i in range(0, len(todo), ATT_PHASE):
            for part in [scores(*t) for t in todo[i:i + ATT_PHASE]]:
                finish(*part)

    blocks([(pi, 0, 0, make_bias(pi, 0)) for pi in range(pairs)])
    if nq > 2:
        for pi in range(pairs):
            b0, b1 = make_bias(pi, -ATT_RADIUS)
            bias_ref[pi, 0] = b0
            bias_ref[pi, 1] = b1
        unroll = max(1, ATT_INFLIGHT // pairs)
        trips, rest = divmod(nq - 2, unroll)

        def interior(first, count):
            todo = []
            for u in range(count):
                q0 = (first + u) * bq
                k0 = q0 - ATT_RADIUS
                if not isinstance(first, int):
                    q0, k0 = pl.multiple_of(q0, bq), pl.multiple_of(k0, ATT_RADIUS)
                todo += [(pi, q0, k0, (bias_ref[pi, 0], bias_ref[pi, 1])) for pi in range(pairs)]
            blocks(todo)

        def body(i, carry):
            interior(1 + i * unroll, unroll)
            return carry

        lax.fori_loop(0, trips, body, 0)
        interior(1 + trips * unroll, rest)
    if nq > 1:
        blocks([(pi, n - bq, n - win, make_bias(pi, bq - win)) for pi in range(pairs)])


def band_attention(slopes, qkv):
    b, _, dil, n, _ = qkv.shape
    bq = min(ATT_QBLOCK, n)
    win = min(bq + 2 * ATT_RADIUS, n)
    pairs = max(1, min(N_SLABS, ATT_ROWS_PER_STEP // n))
    part = lambda c: pl.BlockSpec((1, pairs, 1, n, LANES), lambda bi, r, j, c=c: (bi, c * (N_SLABS // pairs) + j, r, 0, 0))
    out = pl.BlockSpec((1, pairs, 1, n, LANES), lambda bi, r, j: (bi, j, r, 0, 0))
    return pl.pallas_call(
        functools.partial(_band_kernel, n=n, bq=bq, win=win, dil=dil, pairs=pairs),
        out_shape=(jax.ShapeDtypeStruct((b, N_SLABS, dil, n, LANES), BF16),
                   jax.ShapeDtypeStruct((b, N_SLABS, dil, n, LANES), F32)),
        grid=(b, dil, N_SLABS // pairs),
        in_specs=[pl.BlockSpec(memory_space=pltpu.SMEM), part(0), part(1), part(2)],
        out_specs=(out, out),
        scratch_shapes=[pltpu.VMEM((pairs, 2, bq, win), F32)],
        compiler_params=_params("parallel", "parallel", "parallel"),
        name="band_attention",
    )(slopes, qkv, qkv, qkv)


def _trunk(x, mem, p):
    b, l, d = x.shape
    t = b * l
    kv = kv_projection(mem, p["gains"][:, 5], p["xa_w_kv"])
    for i in range(DEPTH):
        g = p["gains"][i]
        xattn = (kv, i, p["xa_w_q"][i], p["xa_w_o"][i], g[4], g[6])
        x = ffn_block(x.reshape(t, d), g[0], g[1], p["ffn_w_in"], p["ffn_w_out"], (i, 0)).reshape(b, l, d)
        j = i // 2
        if i % 2 == 0:
            qvg, gates = rms_matmul(x, g[2], p["hg_w_in"], (j,),
                                    [([(0, d), (3 * d, 2 * d)], BF16), ([(d, 2 * d)], F32)], "planes")
            mix = [hgrn_scan(qvg.reshape(b, -1, l, LANES), gates.reshape(b, -1, l, LANES),
                             p["lower_bounds"][i], p["hg_gnorm"][j])]
            x = mixer_xattn_block(x, mix, (p["hg_w_out"][j], g[3]), xattn)
        else:
            res = [band_attention(p["slopes"], *rms_matmul(x, g[2], p["att_w_in"], (j,),
                                                           [([(3 * d * gi, 3 * d)], BF16)], "planes", dil))
                   for gi, (_, dil) in enumerate(DIL_PATTERNS)]
            mix = [o for o, _ in res] + [lse for _, lse in res]
            x = mixer_xattn_block(x, mix, (p["att_w_out"][j], g[3]))
            x = mixer_xattn_block(x, xattn_params=xattn)
        x = ffn_block(x.reshape(t, d), g[7], g[8], p["ffn_w_in"], p["ffn_w_out"], (i, 1)).reshape(b, l, d)
    return x


def kernel(x_prompt, x_sample, mem_prompt, mem_sample, norm_gains, ffn_w_in, ffn_w_out, hg_w_in, hg_lb_logits, hg_gnorm, hg_w_out, att_w_in, att_w_out, xa_w_q, xa_w_kv, xa_w_o):
    sm = jax.nn.softmax(hg_lb_logits.astype(F32), axis=0)
    lower_bounds = jnp.maximum(jnp.cumsum(sm, axis=0) - sm[0], 0.0)
    p = {
        "gains": norm_gains.astype(F32).reshape(DEPTH, -1, 1, D_MODEL),
        "ffn_w_in": ffn_w_in.astype(BF16),
        "ffn_w_out": ffn_w_out.astype(BF16),
        "hg_w_in": hg_w_in.astype(BF16),
        "lower_bounds": lower_bounds.reshape(DEPTH, HG_HEADS, 1, HG_DK),
        "hg_gnorm": hg_gnorm.astype(F32).reshape(-1, 1, HG_DK),
        "hg_w_out": hg_w_out.astype(BF16),
        "att_w_in": (att_w_in.reshape(-1, D_MODEL, len(DIL_PATTERNS), 3, D_MODEL)
                     * jnp.array([ATT_HD ** -0.5 * LOG2E, 1.0, 1.0], F32)[:, None]).astype(BF16).reshape(att_w_in.shape),
        "att_w_out": att_w_out.astype(BF16),
        "xa_w_q": xa_w_q.astype(BF16),
        "xa_w_kv": xa_w_kv.astype(BF16),
        "xa_w_o": xa_w_o.astype(BF16),
        "slopes": LOG2E * jnp.exp2(-8.0 * jnp.arange(1, ATT_HEADS + 1, dtype=F32) / ATT_HEADS),
    }
    return (_trunk(x_prompt, mem_prompt, p), _trunk(x_sample, mem_sample, p))
```

```python
import functools

import jax
import jax.numpy as jnp
from jax import lax
from jax.experimental import pallas as pl
from jax.experimental.pallas import tpu as pltpu

F32 = jnp.float32
BF16 = jnp.bfloat16

D_MODEL = 1024
LANES = 128
N_SLABS = D_MODEL // LANES
DEPTH = 4
HG_HEADS = 8
HG_DK = 128
DIL_PATTERNS = ((128, 1), (512, 4), (2048, 16))
ATT_HEADS = 16
ATT_HD = 64
ATT_RADIUS = 64
X_HEADS = 4
X_HD = 256
D_FF = 2816
EPS = 1e-6
NEG = -1e30
LOG2E = 1.4426950408889634

HG_CHUNK = 64
HG_GROUP = 16
FF_CHUNK = 256
MM_TILE_N = 1024
SUB_STRIDE = 4
ATT_QBLOCK = 128
ATT_INFLIGHT = 16
ATT_PHASE = 4
ATT_ROWS_PER_STEP = 4096
V7X_VMEM_BYTES = 64 * 1024 * 1024
VMEM_LIMIT = V7X_VMEM_BYTES * 7 // 8

_NT = (((1,), (1,)), ((), ()))
_TN = (((0,), (0,)), ((), ()))


def _params(*sem):
    return pltpu.CompilerParams(dimension_semantics=sem, vmem_limit_bytes=VMEM_LIMIT)


def _rms(x, g):
    return x * lax.rsqrt(jnp.mean(x * x, axis=-1, keepdims=True) + EPS) * g


def _silu(x):
    return x / (1.0 + jnp.exp(-x))


def _row_tile(t, pref):
    while t % pref:
        pref //= 2
    return pref


def _slabs(ref):
    return jnp.concatenate([ref[0, c] for c in range(ref.shape[1])], axis=-1)


def _rms_matmul_kernel(x_ref, g_ref, w_ref, *refs, layout, dil, splits):
    o_refs, scratch = refs[:len(splits)], refs[len(splits):]
    h = _rms(x_ref[0], g_ref[...]).astype(BF16)
    tn = MM_TILE_N
    spt = tn // LANES
    for o_ref, starts in zip(o_refs, splits):
        for j, c0 in enumerate(starts):
            y = jnp.dot(h, w_ref[:, c0:c0 + tn], preferred_element_type=F32)
            if layout == "rows":
                o_ref[0, :, j * tn:(j + 1) * tn] = y.astype(o_ref.dtype)
            elif dil == 1:
                for c in range(spt):
                    o_ref[0, j * spt + c, 0] = y[:, c * LANES:(c + 1) * LANES].astype(o_ref.dtype)
            else:
                y_ref, *mid = scratch
                rows = y.shape[0] // dil
                d1 = SUB_STRIDE if dil > SUB_STRIDE else 1
                d2 = dil // d1
                for c in range(spt):
                    y_ref[c] = y[:, c * LANES:(c + 1) * LANES]
                    for q in range(d1):
                        if d1 == 1:
                            src = y_ref.at[c]
                        else:
                            mid[0][c, q] = y_ref[c, pl.ds(q, rows * d2, stride=d1), :]
                            src = mid[0].at[c, q]
                        for p in range(d2):
                            o_ref[0, j * spt + c, q + d1 * p] = src[pl.ds(p, rows, stride=d2), :].astype(o_ref.dtype)


def _stacked(w, index):
    lead = len(index)
    return pl.BlockSpec((None,) * lead + w.shape[lead:], lambda *_: (*index, 0, 0), pipeline_mode=pl.Buffered(1))


def rms_matmul(x, g, w, w_index, outputs, layout, dil=1, tm=512):
    b, l, d = x.shape
    tn = MM_TILE_N
    tm = _row_tile(l, tm)
    out_shapes, out_specs, splits = [], [], []
    for ranges, dtype in outputs:
        n = sum(cnt for _, cnt in ranges)
        if layout == "rows":
            out_shapes.append(jax.ShapeDtypeStruct((b, l, n), dtype))
            out_specs.append(pl.BlockSpec((1, tm, n), lambda bi, i: (bi, i, 0)))
        else:
            out_shapes.append(jax.ShapeDtypeStruct((b, n // LANES, dil, l // dil, LANES), dtype))
            out_specs.append(pl.BlockSpec((1, n // LANES, dil, tm // dil, LANES), lambda bi, i: (bi, 0, 0, i, 0)))
        splits.append(tuple(c for c0, cnt in ranges for c in range(c0, c0 + cnt, tn)))
    scratch = []
    if dil > 1:
        scratch.append(pltpu.VMEM((tn // LANES, tm, LANES), F32))
    if dil > SUB_STRIDE:
        scratch.append(pltpu.VMEM((tn // LANES, SUB_STRIDE, tm // SUB_STRIDE, LANES), F32))
    return pl.pallas_call(
        functools.partial(_rms_matmul_kernel, layout=layout, dil=dil, splits=tuple(splits)),
        out_shape=tuple(out_shapes),
        grid=(b, l // tm),
        in_specs=[
            pl.BlockSpec((1, tm, d), lambda bi, i: (bi, i, 0)),
            pl.BlockSpec((1, d), lambda bi, i: (0, 0)),
            _stacked(w, w_index),
        ],
        out_specs=tuple(out_specs),
        scratch_shapes=scratch,
        compiler_params=_params("parallel", "parallel"),
        name="rms_matmul",
    )(x, g, w)


def _ffn_kernel(x_ref, g0_ref, g1_ref, wi_ref, wo_ref, o_ref, acc_ref):
    nf = D_FF // FF_CHUNK
    cols = [slice(c * FF_CHUNK, (c + 1) * FF_CHUNK) for c in range(2 * nf)]
    half = x_ref.shape[0] // 2
    rows = (pl.ds(0, half), pl.ds(half, half))

    def norm_in(r):
        return _rms(x_ref[rows[r], :], g0_ref[...]).astype(BF16)

    def norm_out(r):
        o_ref[rows[r], :] = x_ref[rows[r], :] + 0.5 * _rms(acc_ref[rows[r], :], g1_ref[...])

    h = [norm_in(0), None]
    for r in range(2):
        for c in range(nf):
            gate = jnp.dot(h[r], wi_ref[:, cols[c]], preferred_element_type=F32)
            up = jnp.dot(h[r], wi_ref[:, cols[nf + c]], preferred_element_type=F32)
            act = (_silu(gate) * up).astype(BF16)
            down = jnp.dot(act, wo_ref[cols[c], :], preferred_element_type=F32)
            if c == 0:
                acc_ref[rows[r], :] = down
            else:
                acc_ref[rows[r], :] += down
            if c == nf // 2:
                if r == 0:
                    h[1] = norm_in(1)
                else:
                    norm_out(0)
    norm_out(1)


def ffn_block(x, g0, g1, w_in, w_out, w_index, tm=1024):
    t, d = x.shape
    tm = _row_tile(t, tm)
    row = pl.BlockSpec((tm, d), lambda i: (i, 0))
    gain = pl.BlockSpec((1, d), lambda i: (0, 0))
    return pl.pallas_call(
        _ffn_kernel,
        out_shape=jax.ShapeDtypeStruct((t, d), F32),
        grid=(t // tm,),
        in_specs=[row, gain, gain, _stacked(w_in, w_index), _stacked(w_out, w_index)],
        out_specs=row,
        scratch_shapes=[pltpu.VMEM((tm, d), F32)],
        compiler_params=_params("parallel"),
        name="ffn_block",
    )(x, g0, g1, w_in, w_out)


def _token_rows(ref, buf=None, mid=None):
    dil, sub = ref.shape[2], ref.shape[3]
    if dil == 1:
        return jnp.concatenate([ref[0, c, 0] for c in range(N_SLABS)], axis=-1).astype(F32)
    d1 = SUB_STRIDE if dil > SUB_STRIDE else 1
    d2 = dil // d1
    for c in range(N_SLABS):
        for q in range(d1):
            dst = buf.at[c] if d1 == 1 else mid.at[c, q]
            for p in range(d2):
                dst[pl.ds(p, sub, stride=d2), :] = ref[0, c, q + d1 * p].astype(F32)
            if d1 > 1:
                buf[c, pl.ds(q, sub * d2, stride=d1), :] = mid[c, q]
    return jnp.concatenate([buf[c] for c in range(N_SLABS)], axis=-1)


def _merged_groups(o0_ref, o1_ref, o2_ref, l0_ref, l1_ref, l2_ref, bl1, bl2, bo1, bo2, ml2, mo2):
    l0, l1, l2 = _token_rows(l0_ref), _token_rows(l1_ref, bl1), _token_rows(l2_ref, bl2, ml2)
    m = jnp.maximum(jnp.maximum(l0, l1), l2)
    e0, e1, e2 = jnp.exp2(l0 - m), jnp.exp2(l1 - m), jnp.exp2(l2 - m)
    num = e0 * _token_rows(o0_ref) + e1 * _token_rows(o1_ref, bo1) + e2 * _token_rows(o2_ref, bo2, mo2)
    return (num / (e0 + e1 + e2)).astype(BF16)


def _mixer_xattn_kernel(*refs, n_mix, xattn):
    mix_refs, refs = refs[:n_mix], refs[n_mix:]
    if n_mix:
        (wm_ref, gm_ref), refs = refs[:2], refs[2:]
    x_ref, refs = refs[0], refs[1:]
    if xattn:
        (k_ref, v_ref, wq_ref, wo_ref, gq_ref, go_ref), refs = refs[:6], refs[6:]
    o_ref, *scratch = refs
    x = x_ref[0]
    if n_mix:
        mixed = _slabs(mix_refs[0]) if n_mix == 1 else _merged_groups(*mix_refs, *scratch)
        x = x + _rms(jnp.dot(mixed, wm_ref[...], preferred_element_type=F32), gm_ref[...])
    if not xattn:
        o_ref[0] = x
        return
    q = jnp.dot(_rms(x, gq_ref[...]).astype(BF16), wq_ref[...], preferred_element_type=F32).astype(BF16)
    k = k_ref[0]
    v = v_ref[0]
    cols = [slice(hd * X_HD, (hd + 1) * X_HD) for hd in range(X_HEADS)]
    scores = [lax.dot_general(q[:, sl], k[:, sl], _NT, preferred_element_type=F32) * (X_HD ** -0.5) for sl in cols]
    probs = [jnp.exp(s - jnp.max(s, axis=-1, keepdims=True)) for s in scores]
    heads = [jnp.dot(p.astype(BF16), v[:, sl], preferred_element_type=F32) / jnp.sum(p, axis=-1, keepdims=True)
             for p, sl in zip(probs, cols)]
    c = jnp.dot(jnp.concatenate(heads, axis=-1).astype(BF16), wo_ref[...], preferred_element_type=F32)
    o_ref[0] = x + _rms(c, go_ref[...])


def mixer_xattn_block(x, mix=(), mix_params=(), xattn_params=()):
    b, l, d = x.shape
    merge = len(mix) > 1
    tm = _row_tile(l, 512)
    row = pl.BlockSpec((1, tm, d), lambda bi, i: (bi, i, 0))
    full = pl.BlockSpec((d, d), lambda bi, i: (0, 0), pipeline_mode=pl.Buffered(1))
    gain = pl.BlockSpec((1, d), lambda bi, i: (0, 0))
    specs, args, scratch = [], [], []
    if merge:
        specs += [pl.BlockSpec((1, N_SLABS, a.shape[2], tm // a.shape[2], LANES), lambda bi, i: (bi, 0, 0, i, 0))
                  for a in mix]
        scratch = ([pltpu.VMEM((N_SLABS, tm, LANES), F32)] * 4
                   + [pltpu.VMEM((N_SLABS, SUB_STRIDE, tm // SUB_STRIDE, LANES), F32)] * 2)
    elif mix:
        specs += [pl.BlockSpec((1, N_SLABS, tm, LANES), lambda bi, i: (bi, 0, i, 0))]
    if mix:
        specs += [full, gain]
        args += [*mix, *mix_params]
    specs.append(row)
    args.append(x)
    if xattn_params:
        kv, layer, w_q, w_o, g_q, g_o = xattn_params
        m = kv.shape[2]
        specs += [pl.BlockSpec((None, 1, m, d), lambda bi, i: (layer, bi, 0, 0)),
                  pl.BlockSpec((None, 1, m, d), lambda bi, i: (layer, bi, 0, 1)),
                  full, full, gain, gain]
        args += [kv, kv, w_q, w_o, g_q, g_o]
    return pl.pallas_call(
        functools.partial(_mixer_xattn_kernel, n_mix=len(mix), xattn=bool(xattn_params)),
        out_shape=jax.ShapeDtypeStruct((b, l, d), F32),
        grid=(b, l // tm),
        in_specs=specs,
        out_specs=row,
        scratch_shapes=scratch,
        compiler_params=_params("parallel", "parallel"),
        name="mixer_xattn_block",
    )(*args)


def kv_projection(mem, gains, w):
    b, m, d = mem.shape
    depth, _, n = w.shape
    return pl.pallas_call(
        functools.partial(_rms_matmul_kernel, layout="rows", dil=1, splits=(tuple(range(0, n, MM_TILE_N)),)),
        out_shape=jax.ShapeDtypeStruct((depth, b, m, n), BF16),
        grid=(depth, b),
        in_specs=[
            pl.BlockSpec((1, m, d), lambda li, bi: (bi, 0, 0)),
            pl.BlockSpec((None, 1, d), lambda li, bi: (li, 0, 0)),
            pl.BlockSpec((None, d, n), lambda li, bi: (li, 0, 0)),
        ],
        out_specs=pl.BlockSpec((None, 1, m, n), lambda li, bi: (li, bi, 0, 0)),
        compiler_params=_params("parallel", "parallel"),
        name="kv_projection",
    )(mem, gains, w)


def _hgrn_kernel(q_ref, zf_ref, zb_ref, v_ref, g_ref, lb_ref, gn_ref, o_ref, of_ref, ob_ref, stf_ref, stb_ref, *, seq):
    c_len = HG_CHUNK
    nc = seq // c_len
    half = c_len // 2
    lb = lb_ref[0]
    log_lb = jnp.log(lb)
    log_1m_lb = jnp.log1p(-lb)
    row = lax.broadcasted_iota(jnp.int32, (c_len, c_len), 0)
    col = lax.broadcasted_iota(jnp.int32, (c_len, c_len), 1)
    same_half = (row < half) == (col < half)
    lower = (col <= row) & same_half
    upper = (col >= row) & same_half
    lower_sum = (col <= row).astype(BF16)
    upper_sum = (col >= row).astype(BF16)

    def halves(first, second):
        return jnp.concatenate([jnp.broadcast_to(first, (half, HG_DK)), jnp.broadcast_to(second, (half, HG_DK))], axis=0)

    def gate(z):
        e = jnp.exp(-jnp.abs(z))
        log_sig = jnp.minimum(z, 0.0) - jnp.log(1.0 + e)
        t = log_1m_lb + log_sig
        logf = jnp.maximum(log_lb, t) + jnp.log(1.0 + jnp.exp(-jnp.abs(log_lb - t)))
        sig_neg = jnp.where(z >= 0.0, e, 1.0) / (1.0 + e)
        return logf, (1.0 - lb) * sig_neg

    grp = _row_tile(nc, HG_GROUP)
    ng = nc // grp
    quarter = half // 2
    scans = ((zf_ref, stf_ref, of_ref, lower_sum, lower, (quarter - 1, half - 1, half + quarter - 1, c_len - 1), False),
             (zb_ref, stb_ref, ob_ref, upper_sum, upper, (quarter, half, half + quarter, 0), True))

    stf_ref[...] = jnp.zeros_like(stf_ref)
    stb_ref[...] = jnp.zeros_like(stb_ref)

    def scan(i, carry):
        steps = []
        for z_ref, st_ref, out_ref, sum_mat, causal, ref_rows, rev in scans:
            g0 = (ng - 1 - i) if rev else i
            sls = [pl.ds(pl.multiple_of((g0 * grp + j) * c_len, c_len), c_len) for j in range(grp)]
            gates = [gate(z_ref[0, 0, sl, :]) for sl in sls]
            logf = jnp.concatenate([lf for lf, _ in gates], axis=1)
            hi = logf.astype(BF16)
            lo = (logf - hi.astype(F32)).astype(BF16)
            cums = jnp.dot(jnp.concatenate([sum_mat, sum_mat], axis=1), jnp.concatenate([hi, lo], axis=0),
                           preferred_element_type=F32)
            for j in (reversed(range(grp)) if rev else range(grp)):
                steps.append(dict(sl=sls[j], k=gates[j][1], cum=cums[:, j * HG_DK:(j + 1) * HG_DK], causal=causal,
                                  ref_rows=ref_rows, rev=rev, st_ref=st_ref, out_ref=out_ref))
        order = [steps[d * grp + j] for j in range(grp) for d in range(2)]
        zeros = jnp.zeros((half, HG_DK), F32)
        for e in order:
            cum = e["cum"]
            m_a, m_mid, m_b, tot = (cum[r:r + 1, :] for r in e["ref_rows"])
            q = _silu(q_ref[0, 0, e["sl"], :].astype(F32))
            e["v"] = v_ref[0, 0, e["sl"], :]
            ref = halves(m_a, m_b)
            qd = q * jnp.exp(cum - ref)
            kd = e["k"] * jnp.exp(ref - cum)
            e["qe"] = (qd * halves(jnp.exp(m_a), jnp.exp(m_b))).astype(BF16)
            ke = (kd * halves(jnp.exp(tot - m_a), jnp.exp(tot - m_b))).astype(BF16)
            e["decay"] = jnp.exp(tot)
            if e["rev"]:
                qx = jnp.concatenate([qd[:half] * jnp.exp(m_a - m_mid), zeros], axis=0)
                kx = jnp.concatenate([zeros, kd[half:] * jnp.exp(m_mid - m_b)], axis=0)
            else:
                qx = jnp.concatenate([zeros, qd[half:] * jnp.exp(m_b - m_mid)], axis=0)
                kx = jnp.concatenate([kd[:half] * jnp.exp(m_mid - m_a), zeros], axis=0)
            e["s_in"] = lax.dot_general(qd.astype(BF16), kd.astype(BF16), _NT, preferred_element_type=F32)
            e["s_x"] = lax.dot_general(qx.astype(BF16), kx.astype(BF16), _NT, preferred_element_type=F32)
            e["upd"] = lax.dot_general(e["v"], ke, _TN, preferred_element_type=F32)
        for e in order:
            s = (jnp.where(e["causal"], e["s_in"], 0.0) + e["s_x"]).astype(BF16)
            e["o"] = jnp.dot(s, e["v"], preferred_element_type=F32)
        states = [st_ref[...] for _, st_ref, *_ in scans]
        for idx, e in enumerate(order):
            st = states[idx % 2]
            o = e["o"] + lax.dot_general(e["qe"], st.astype(BF16), _NT, preferred_element_type=F32)
            e["out_ref"][e["sl"], :] = o
            states[idx % 2] = st * e["decay"] + e["upd"]
        for st, (_, st_ref, *_) in zip(states, scans):
            st_ref[...] = st
        return carry

    lax.fori_loop(0, ng, scan, 0)

    rows = _row_tile(seq, 512)

    def finish(i, carry):
        sl = pl.ds(pl.multiple_of(i * rows, rows), rows)
        o = of_ref[sl, :] + ob_ref[sl, :]
        o = o * lax.rsqrt(jnp.mean(o * o, axis=-1, keepdims=True) + EPS)
        o_ref[0, 0, sl, :] = (o * gn_ref[...] * _silu(g_ref[0, 0, sl, :].astype(F32))).astype(o_ref.dtype)
        return carry

    lax.fori_loop(0, seq // rows, finish, 0)


def hgrn_scan(qvg, gates, lb, gnorm):
    b, _, l, _ = qvg.shape
    part = lambda p: pl.BlockSpec((1, 1, l, HG_DK), lambda i, h, p=p: (i, p * HG_HEADS + h, 0, 0))
    return pl.pallas_call(
        functools.partial(_hgrn_kernel, seq=l),
        out_shape=jax.ShapeDtypeStruct((b, HG_HEADS, l, HG_DK), BF16),
        grid=(b, HG_HEADS),
        in_specs=[part(0), part(0), part(1), part(1), part(2),
                  pl.BlockSpec((1, 1, HG_DK), lambda i, h: (h, 0, 0)),
                  pl.BlockSpec((1, HG_DK), lambda i, h: (0, 0))],
        out_specs=pl.BlockSpec((1, 1, l, HG_DK), lambda i, h: (i, h, 0, 0)),
        scratch_shapes=[pltpu.VMEM((l, HG_DK), F32), pltpu.VMEM((l, HG_DK), F32),
                        pltpu.VMEM((HG_DK, HG_DK), F32), pltpu.VMEM((HG_DK, HG_DK), F32)],
        compiler_params=_params("parallel", "parallel"),
        name="hgrn_scan",
    )(qvg, gates, gates, qvg, qvg, lb, gnorm)


def _band_kernel(slope_ref, q_ref, k_ref, v_ref, o_ref, l_ref, bias_ref, *, n, bq, win, dil, pairs):
    nq = n // bq
    lane = lax.broadcasted_iota(jnp.int32, (1, LANES), 1)
    head0 = lane < ATT_HD
    qsel = (jnp.where(head0, 1.0, 0.0).astype(BF16), jnp.where(head0, 0.0, 1.0).astype(BF16))
    ones0 = jnp.broadcast_to(jnp.where(head0, 1.0, 0.0).astype(BF16), (win, LANES))
    ones1 = jnp.broadcast_to(jnp.where(head0, 0.0, 1.0).astype(BF16), (win, LANES))
    delta = (lax.broadcasted_iota(jnp.int32, (bq, win), 1) - lax.broadcasted_iota(jnp.int32, (bq, win), 0))

    def make_bias(pi, offset):
        rel = jnp.abs(delta + offset)
        base = jnp.where(rel <= ATT_RADIUS, -(dil * rel).astype(F32), NEG)
        pair = pl.program_id(2) * pairs + pi
        return [slope_ref[2 * pair + h] * base for h in range(2)]

    def scores(pi, q0, k0, bias):
        q = q_ref[0, pi, 0, pl.ds(q0, bq), :]
        k = k_ref[0, pi, 0, pl.ds(k0, win), :]
        ps, ms = [], []
        for h in range(2):
            s = lax.dot_general(q * qsel[h], k, _NT, preferred_element_type=F32) + bias[h]
            m = jnp.max(s, axis=-1, keepdims=True)
            ps.append(jnp.exp2(s - m).astype(BF16))
            ms.append(m)
        return pi, q0, k0, ps, ms

    def finish(pi, q0, k0, ps, ms):
        v = v_ref[0, pi, 0, pl.ds(k0, win), :]
        rhs = jnp.concatenate([
            jnp.concatenate([jnp.where(head0, v, 0), ones0], axis=1),
            jnp.concatenate([jnp.where(head0, 0, v), ones1], axis=1)], axis=0)
        r = jnp.dot(jnp.concatenate(ps, axis=1), rhs, preferred_element_type=F32)
        den = r[:, LANES:]
        o_ref[0, pi, 0, pl.ds(q0, bq), :] = (r[:, :LANES] / den).astype(o_ref.dtype)
        l_ref[0, pi, 0, pl.ds(q0, bq), :] = jnp.where(head0, ms[0], ms[1]) + jnp.log2(den)

    def blocks(todo):
        for i in range(0, len(todo), ATT_PHASE):
            for part in [scores(*t) for t in todo[i:i + ATT_PHASE]]:
                finish(*part)

    blocks([(pi, 0, 0, make_bias(pi, 0)) for pi in range(pairs)])
    if nq > 2:
        for pi in range(pairs):
            b0, b1 = make_bias(pi, -ATT_RADIUS)
            bias_ref[pi, 0] = b0
            bias_ref[pi, 1] = b1
        unroll = max(1, ATT_INFLIGHT // pairs)
        trips, rest = divmod(nq - 2, unroll)

        def interior(first, count):
            todo = []
            for u in range(count):
                q0 = (first + u) * bq
                k0 = q0 - ATT_RADIUS
                if not isinstance(first, int):
                    q0, k0 = pl.multiple_of(q0, bq), pl.multiple_of(k0, ATT_RADIUS)
                todo += [(pi, q0, k0, (bias_ref[pi, 0], bias_ref[pi, 1])) for pi in range(pairs)]
            blocks(todo)

        def body(i, carry):
            interior(1 + i * unroll, unroll)
            return carry

        lax.fori_loop(0, trips, body, 0)
        interior(1 + trips * unroll, rest)
    if nq > 1:
        blocks([(pi, n - bq, n - win, make_bias(pi, bq - win)) for pi in range(pairs)])


def band_attention(slopes, qkv):
    b, _, dil, n, _ = qkv.shape
    bq = min(ATT_QBLOCK, n)
    win = min(bq + 2 * ATT_RADIUS, n)
    pairs = max(1, min(N_SLABS, ATT_ROWS_PER_STEP // n))
    part = lambda c: pl.BlockSpec((1, pairs, 1, n, LANES), lambda bi, r, j, c=c: (bi, c * (N_SLABS // pairs) + j, r, 0, 0))
    out = pl.BlockSpec((1, pairs, 1, n, LANES), lambda bi, r, j: (bi, j, r, 0, 0))
    return pl.pallas_call(
        functools.partial(_band_kernel, n=n, bq=bq, win=win, dil=dil, pairs=pairs),
        out_shape=(jax.ShapeDtypeStruct((b, N_SLABS, dil, n, LANES), BF16),
                   jax.ShapeDtypeStruct((b, N_SLABS, dil, n, LANES), F32)),
        grid=(b, dil, N_SLABS // pairs),
        in_specs=[pl.BlockSpec(memory_space=pltpu.SMEM), part(0), part(1), part(2)],
        out_specs=(out, out),
        scratch_shapes=[pltpu.VMEM((pairs, 2, bq, win), F32)],
        compiler_params=_params("parallel", "parallel", "parallel"),
        name="band_attention",
    )(slopes, qkv, qkv, qkv)


def _trunk(x, mem, p):
    b, l, d = x.shape
    t = b * l
    kv = kv_projection(mem, p["gains"][:, 5], p["xa_w_kv"])
    for i in range(DEPTH):
        g = p["gains"][i]
        xattn = (kv, i, p["xa_w_q"][i], p["xa_w_o"][i], g[4], g[6])
        x = ffn_block(x.reshape(t, d), g[0], g[1], p["ffn_w_in"], p["ffn_w_out"], (i, 0)).reshape(b, l, d)
        j = i // 2
        if i % 2 == 0:
            qvg, gates = rms_matmul(x, g[2], p["hg_w_in"], (j,),
                                    [([(0, d), (3 * d, 2 * d)], BF16), ([(d, 2 * d)], F32)], "planes")
            mix = [hgrn_scan(qvg.reshape(b, -1, l, LANES), gates.reshape(b, -1, l, LANES),
                             p["lower_bounds"][i], p["hg_gnorm"][j])]
            x = mixer_xattn_block(x, mix, (p["hg_w_out"][j], g[3]), xattn)
        else:
            res = [band_attention(p["slopes"], *rms_matmul(x, g[2], p["att_w_in"], (j,),
                                                           [([(3 * d * gi, 3 * d)], BF16)], "planes", dil))
                   for gi, (_, dil) in enumerate(DIL_PATTERNS)]
            mix = [o for o, _ in res] + [lse for _, lse in res]
            x = mixer_xattn_block(x, mix, (p["att_w_out"][j], g[3]))
            x = mixer_xattn_block(x, xattn_params=xattn)
        x = ffn_block(x.reshape(t, d), g[7], g[8], p["ffn_w_in"], p["ffn_w_out"], (i, 1)).reshape(b, l, d)
    return x


def kernel(x_prompt, x_sample, mem_prompt, mem_sample, norm_gains, ffn_w_in, ffn_w_out, hg_w_in, hg_lb_logits, hg_gnorm, hg_w_out, att_w_in, att_w_out, xa_w_q, xa_w_kv, xa_w_o):
    sm = jax.nn.softmax(hg_lb_logits.astype(F32), axis=0)
    lower_bounds = jnp.maximum(jnp.cumsum(sm, axis=0) - sm[0], 0.0)
    p = {
        "gains": norm_gains.astype(F32).reshape(DEPTH, -1, 1, D_MODEL),
        "ffn_w_in": ffn_w_in.astype(BF16),
        "ffn_w_out": ffn_w_out.astype(BF16),
        "hg_w_in": hg_w_in.astype(BF16),
        "lower_bounds": lower_bounds.reshape(DEPTH, HG_HEADS, 1, HG_DK),
        "hg_gnorm": hg_gnorm.astype(F32).reshape(-1, 1, HG_DK),
        "hg_w_out": hg_w_out.astype(BF16),
        "att_w_in": (att_w_in.reshape(-1, D_MODEL, len(DIL_PATTERNS), 3, D_MODEL)
                     * jnp.array([ATT_HD ** -0.5 * LOG2E, 1.0, 1.0], F32)[:, None]).astype(BF16).reshape(att_w_in.shape),
        "att_w_out": att_w_out.astype(BF16),
        "xa_w_q": xa_w_q.astype(BF16),
        "xa_w_kv": xa_w_kv.astype(BF16),
        "xa_w_o": xa_w_o.astype(BF16),
        "slopes": LOG2E * jnp.exp2(-8.0 * jnp.arange(1, ATT_HEADS + 1, dtype=F32) / ATT_HEADS),
    }
    return (_trunk(x_prompt, mem_prompt, p), _trunk(x_sample, mem_sample, p))
```

```python
import functools

import jax
import jax.numpy as jnp
from jax import lax
from jax.experimental import pallas as pl
from jax.experimental.pallas import tpu as pltpu

F32 = jnp.float32
BF16 = jnp.bfloat16

D_MODEL = 1024
LANES = 128
N_SLABS = D_MODEL // LANES
DEPTH = 4
HG_HEADS = 8
HG_DK = 128
DIL_PATTERNS = ((128, 1), (512, 4), (2048, 16))
ATT_HEADS = 16
ATT_HD = 64
ATT_RADIUS = 64
X_HEADS = 4
X_HD = 256
D_FF = 2816
EPS = 1e-6
NEG = -1e30
LOG2E = 1.4426950408889634

HG_CHUNK = 64
HG_GROUP = 32
FF_CHUNK = 256
MM_TILE_N = 1024
SUB_STRIDE = 4
ATT_QBLOCK = 128
ATT_INFLIGHT = 16
ATT_PHASE = 4
ATT_ROWS_PER_STEP = 4096
V7X_VMEM_BYTES = 64 * 1024 * 1024
VMEM_LIMIT = V7X_VMEM_BYTES * 7 // 8

_NT = (((1,), (1,)), ((), ()))
_TN = (((0,), (0,)), ((), ()))


def _params(*sem):
    return pltpu.CompilerParams(dimension_semantics=sem, vmem_limit_bytes=VMEM_LIMIT)


def _rms(x, g):
    return x * lax.rsqrt(jnp.mean(x * x, axis=-1, keepdims=True) + EPS) * g


def _silu(x):
    return x / (1.0 + jnp.exp(-x))


def _row_tile(t, pref):
    while t % pref:
        pref //= 2
    return pref


def _slabs(ref):
    return jnp.concatenate([ref[0, c] for c in range(ref.shape[1])], axis=-1)


def _rms_matmul_kernel(x_ref, g_ref, w_ref, *refs, layout, dil, splits):
    o_refs, scratch = refs[:len(splits)], refs[len(splits):]
    h = _rms(x_ref[0], g_ref[...]).astype(BF16)
    tn = MM_TILE_N
    spt = tn // LANES
    for o_ref, starts in zip(o_refs, splits):
        for j, c0 in enumerate(starts):
            y = jnp.dot(h, w_ref[:, c0:c0 + tn], preferred_element_type=F32)
            if layout == "rows":
                o_ref[0, :, j * tn:(j + 1) * tn] = y.astype(o_ref.dtype)
            elif dil == 1:
                for c in range(spt):
                    o_ref[0, j * spt + c, 0] = y[:, c * LANES:(c + 1) * LANES].astype(o_ref.dtype)
            else:
                y_ref, *mid = scratch
                rows = y.shape[0] // dil
                d1 = SUB_STRIDE if dil > SUB_STRIDE else 1
                d2 = dil // d1
                for c in range(spt):
                    y_ref[c] = y[:, c * LANES:(c + 1) * LANES]
                    for q in range(d1):
                        if d1 == 1:
                            src = y_ref.at[c]
                        else:
                            mid[0][c, q] = y_ref[c, pl.ds(q, rows * d2, stride=d1), :]
                            src = mid[0].at[c, q]
                        for p in range(d2):
                            o_ref[0, j * spt + c, q + d1 * p] = src[pl.ds(p, rows, stride=d2), :].astype(o_ref.dtype)


def _stacked(w, index):
    lead = len(index)
    return pl.BlockSpec((None,) * lead + w.shape[lead:], lambda *_: (*index, 0, 0), pipeline_mode=pl.Buffered(1))


def rms_matmul(x, g, w, w_index, outputs, layout, dil=1, tm=512):
    b, l, d = x.shape
    tn = MM_TILE_N
    tm = _row_tile(l, tm)
    out_shapes, out_specs, splits = [], [], []
    for ranges, dtype in outputs:
        n = sum(cnt for _, cnt in ranges)
        if layout == "rows":
            out_shapes.append(jax.ShapeDtypeStruct((b, l, n), dtype))
            out_specs.append(pl.BlockSpec((1, tm, n), lambda bi, i: (bi, i, 0)))
        else:
            out_shapes.append(jax.ShapeDtypeStruct((b, n // LANES, dil, l // dil, LANES), dtype))
            out_specs.append(pl.BlockSpec((1, n // LANES, dil, tm // dil, LANES), lambda bi, i: (bi, 0, 0, i, 0)))
        splits.append(tuple(c for c0, cnt in ranges for c in range(c0, c0 + cnt, tn)))
    scratch = []
    if dil > 1:
        scratch.append(pltpu.VMEM((tn // LANES, tm, LANES), F32))
    if dil > SUB_STRIDE:
        scratch.append(pltpu.VMEM((tn // LANES, SUB_STRIDE, tm // SUB_STRIDE, LANES), F32))
    return pl.pallas_call(
        functools.partial(_rms_matmul_kernel, layout=layout, dil=dil, splits=tuple(splits)),
        out_shape=tuple(out_shapes),
        grid=(b, l // tm),
        in_specs=[
            pl.BlockSpec((1, tm, d), lambda bi, i: (bi, i, 0)),
            pl.BlockSpec((1, d), lambda bi, i: (0, 0)),
            _stacked(w, w_index),
        ],
        out_specs=tuple(out_specs),
        scratch_shapes=scratch,
        compiler_params=_params("parallel", "parallel"),
        name="rms_matmul",
    )(x, g, w)


def _ffn_kernel(x_ref, g0_ref, g1_ref, wi_ref, wo_ref, o_ref, acc_ref):
    nf = D_FF // FF_CHUNK
    cols = [slice(c * FF_CHUNK, (c + 1) * FF_CHUNK) for c in range(2 * nf)]
    half = x_ref.shape[0] // 2
    rows = (pl.ds(0, half), pl.ds(half, half))

    def norm_in(r):
        return _rms(x_ref[rows[r], :], g0_ref[...]).astype(BF16)

    def norm_out(r):
        o_ref[rows[r], :] = x_ref[rows[r], :] + 0.5 * _rms(acc_ref[rows[r], :], g1_ref[...])

    h = [norm_in(0), None]
    for r in range(2):
        for c in range(nf):
            gate = jnp.dot(h[r], wi_ref[:, cols[c]], preferred_element_type=F32)
            up = jnp.dot(h[r], wi_ref[:, cols[nf + c]], preferred_element_type=F32)
            act = (_silu(gate) * up).astype(BF16)
            down = jnp.dot(act, wo_ref[cols[c], :], preferred_element_type=F32)
            if c == 0:
                acc_ref[rows[r], :] = down
            else:
                acc_ref[rows[r], :] += down
            if c == nf // 2:
                if r == 0:
                    h[1] = norm_in(1)
                else:
                    norm_out(0)
    norm_out(1)


def ffn_block(x, g0, g1, w_in, w_out, w_index, tm=1024):
    t, d = x.shape
    tm = _row_tile(t, tm)
    row = pl.BlockSpec((tm, d), lambda i: (i, 0))
    gain = pl.BlockSpec((1, d), lambda i: (0, 0))
    return pl.pallas_call(
        _ffn_kernel,
        out_shape=jax.ShapeDtypeStruct((t, d), F32),
        grid=(t // tm,),
        in_specs=[row, gain, gain, _stacked(w_in, w_index), _stacked(w_out, w_index)],
        out_specs=row,
        scratch_shapes=[pltpu.VMEM((tm, d), F32)],
        compiler_params=_params("parallel"),
        name="ffn_block",
    )(x, g0, g1, w_in, w_out)


def _token_rows(ref, buf=None, mid=None):
    dil, sub = ref.shape[2], ref.shape[3]
    if dil == 1:
        return jnp.concatenate([ref[0, c, 0] for c in range(N_SLABS)], axis=-1).astype(F32)
    d1 = SUB_STRIDE if dil > SUB_STRIDE else 1
    d2 = dil // d1
    for c in range(N_SLABS):
        for q in range(d1):
            dst = buf.at[c] if d1 == 1 else mid.at[c, q]
            for p in range(d2):
                dst[pl.ds(p, sub, stride=d2), :] = ref[0, c, q + d1 * p].astype(F32)
            if d1 > 1:
                buf[c, pl.ds(q, sub * d2, stride=d1), :] = mid[c, q]
    return jnp.concatenate([buf[c] for c in range(N_SLABS)], axis=-1)


def _merged_groups(o0_ref, o1_ref, o2_ref, l0_ref, l1_ref, l2_ref, bl1, bl2, bo1, bo2, ml2, mo2):
    l0, l1, l2 = _token_rows(l0_ref), _token_rows(l1_ref, bl1), _token_rows(l2_ref, bl2, ml2)
    m = jnp.maximum(jnp.maximum(l0, l1), l2)
    e0, e1, e2 = jnp.exp2(l0 - m), jnp.exp2(l1 - m), jnp.exp2(l2 - m)
    num = e0 * _token_rows(o0_ref) + e1 * _token_rows(o1_ref, bo1) + e2 * _token_rows(o2_ref, bo2, mo2)
    return (num / (e0 + e1 + e2)).astype(BF16)


def _mixer_xattn_kernel(*refs, n_mix, xattn):
    mix_refs, refs = refs[:n_mix], refs[n_mix:]
    if n_mix:
        (wm_ref, gm_ref), refs = refs[:2], refs[2:]
    x_ref, refs = refs[0], refs[1:]
    if xattn:
        (k_ref, v_ref, wq_ref, wo_ref, gq_ref, go_ref), refs = refs[:6], refs[6:]
    o_ref, *scratch = refs
    x = x_ref[0]
    if n_mix:
        mixed = _slabs(mix_refs[0]) if n_mix == 1 else _merged_groups(*mix_refs, *scratch)
        x = x + _rms(jnp.dot(mixed, wm_ref[...], preferred_element_type=F32), gm_ref[...])
    if not xattn:
        o_ref[0] = x
        return
    q = jnp.dot(_rms(x, gq_ref[...]).astype(BF16), wq_ref[...], preferred_element_type=F32).astype(BF16)
    k = k_ref[0]
    v = v_ref[0]
    cols = [slice(hd * X_HD, (hd + 1) * X_HD) for hd in range(X_HEADS)]
    scores = [lax.dot_general(q[:, sl], k[:, sl], _NT, preferred_element_type=F32) * (X_HD ** -0.5) for sl in cols]
    probs = [jnp.exp(s - jnp.max(s, axis=-1, keepdims=True)) for s in scores]
    heads = [jnp.dot(p.astype(BF16), v[:, sl], preferred_element_type=F32) / jnp.sum(p, axis=-1, keepdims=True)
             for p, sl in zip(probs, cols)]
    c = jnp.dot(jnp.concatenate(heads, axis=-1).astype(BF16), wo_ref[...], preferred_element_type=F32)
    o_ref[0] = x + _rms(c, go_ref[...])


def mixer_xattn_block(x, mix=(), mix_params=(), xattn_params=()):
    b, l, d = x.shape
    merge = len(mix) > 1
    tm = _row_tile(l, 512)
    row = pl.BlockSpec((1, tm, d), lambda bi, i: (bi, i, 0))
    full = pl.BlockSpec((d, d), lambda bi, i: (0, 0), pipeline_mode=pl.Buffered(1))
    gain = pl.BlockSpec((1, d), lambda bi, i: (0, 0))
    specs, args, scratch = [], [], []
    if merge:
        specs += [pl.BlockSpec((1, N_SLABS, a.shape[2], tm // a.shape[2], LANES), lambda bi, i: (bi, 0, 0, i, 0))
                  for a in mix]
        scratch = ([pltpu.VMEM((N_SLABS, tm, LANES), F32)] * 4
                   + [pltpu.VMEM((N_SLABS, SUB_STRIDE, tm // SUB_STRIDE, LANES), F32)] * 2)
    elif mix:
        specs += [pl.BlockSpec((1, N_SLABS, tm, LANES), lambda bi, i: (bi, 0, i, 0))]
    if mix:
        specs += [full, gain]
        args += [*mix, *mix_params]
    specs.append(row)
    args.append(x)
    if xattn_params:
        kv, layer, w_q, w_o, g_q, g_o = xattn_params
        m = kv.shape[2]
        specs += [pl.BlockSpec((None, 1, m, d), lambda bi, i: (layer, bi, 0, 0)),
                  pl.BlockSpec((None, 1, m, d), lambda bi, i: (layer, bi, 0, 1)),
                  full, full, gain, gain]
        args += [kv, kv, w_q, w_o, g_q, g_o]
    return pl.pallas_call(
        functools.partial(_mixer_xattn_kernel, n_mix=len(mix), xattn=bool(xattn_params)),
        out_shape=jax.ShapeDtypeStruct((b, l, d), F32),
        grid=(b, l // tm),
        in_specs=specs,
        out_specs=row,
        scratch_shapes=scratch,
        compiler_params=_params("parallel", "parallel"),
        name="mixer_xattn_block",
    )(*args)


def kv_projection(mem, gains, w):
    b, m, d = mem.shape
    depth, _, n = w.shape
    return pl.pallas_call(
        functools.partial(_rms_matmul_kernel, layout="rows", dil=1, splits=(tuple(range(0, n, MM_TILE_N)),)),
        out_shape=jax.ShapeDtypeStruct((depth, b, m, n), BF16),
        grid=(depth, b),
        in_specs=[
            pl.BlockSpec((1, m, d), lambda li, bi: (bi, 0, 0)),
            pl.BlockSpec((None, 1, d), lambda li, bi: (li, 0, 0)),
            pl.BlockSpec((None, d, n), lambda li, bi: (li, 0, 0)),
        ],
        out_specs=pl.BlockSpec((None, 1, m, n), lambda li, bi: (li, bi, 0, 0)),
        compiler_params=_params("parallel", "parallel"),
        name="kv_projection",
    )(mem, gains, w)


def _hgrn_kernel(q_ref, zf_ref, zb_ref, v_ref, g_ref, lb_ref, gn_ref, o_ref, of_ref, ob_ref, stf_ref, stb_ref, *, seq):
    c_len = HG_CHUNK
    nc = seq // c_len
    half = c_len // 2
    lb = lb_ref[0]
    log_lb = jnp.log(lb)
    log_1m_lb = jnp.log1p(-lb)
    row = lax.broadcasted_iota(jnp.int32, (c_len, c_len), 0)
    col = lax.broadcasted_iota(jnp.int32, (c_len, c_len), 1)
    same_half = (row < half) == (col < half)
    lower = (col <= row) & same_half
    upper = (col >= row) & same_half
    lower_sum = (col <= row).astype(BF16)
    upper_sum = (col >= row).astype(BF16)

    def halves(first, second):
        return jnp.concatenate([jnp.broadcast_to(first, (half, HG_DK)), jnp.broadcast_to(second, (half, HG_DK))], axis=0)

    def gate(z):
        e = jnp.exp(-jnp.abs(z))
        log_sig = jnp.minimum(z, 0.0) - jnp.log(1.0 + e)
        t = log_1m_lb + log_sig
        logf = jnp.maximum(log_lb, t) + jnp.log(1.0 + jnp.exp(-jnp.abs(log_lb - t)))
        sig_neg = jnp.where(z >= 0.0, e, 1.0) / (1.0 + e)
        return logf, (1.0 - lb) * sig_neg

    grp = _row_tile(nc, HG_GROUP)
    ng = nc // grp
    quarter = half // 2
    scans = ((zf_ref, stf_ref, of_ref, lower_sum, lower, (quarter - 1, half - 1, half + quarter - 1, c_len - 1), False),
             (zb_ref, stb_ref, ob_ref, upper_sum, upper, (quarter, half, half + quarter, 0), True))

    stf_ref[...] = jnp.zeros_like(stf_ref)
    stb_ref[...] = jnp.zeros_like(stb_ref)

    def scan(i, carry):
        steps = []
        for z_ref, st_ref, out_ref, sum_mat, causal, ref_rows, rev in scans:
            g0 = (ng - 1 - i) if rev else i
            sls = [pl.ds(pl.multiple_of((g0 * grp + j) * c_len, c_len), c_len) for j in range(grp)]
            gates = [gate(z_ref[0, 0, sl, :]) for sl in sls]
            logf = jnp.concatenate([lf for lf, _ in gates], axis=1)
            hi = logf.astype(BF16)
            lo = (logf - hi.astype(F32)).astype(BF16)
            cums = jnp.dot(jnp.concatenate([sum_mat, sum_mat], axis=1), jnp.concatenate([hi, lo], axis=0),
                           preferred_element_type=F32)
            for j in (reversed(range(grp)) if rev else range(grp)):
                steps.append(dict(sl=sls[j], k=gates[j][1], cum=cums[:, j * HG_DK:(j + 1) * HG_DK], causal=causal,
                                  ref_rows=ref_rows, rev=rev, st_ref=st_ref, out_ref=out_ref))
        order = [steps[d * grp + j] for j in range(grp) for d in range(2)]
        zeros = jnp.zeros((half, HG_DK), F32)
        for e in order:
            cum = e["cum"]
            m_a, m_mid, m_b, tot = (cum[r:r + 1, :] for r in e["ref_rows"])
            q = _silu(q_ref[0, 0, e["sl"], :].astype(F32))
            e["v"] = v_ref[0, 0, e["sl"], :]
            ref = halves(m_a, m_b)
            qd = q * jnp.exp(cum - ref)
            kd = e["k"] * jnp.exp(ref - cum)
            e["qe"] = (qd * halves(jnp.exp(m_a), jnp.exp(m_b))).astype(BF16)
            ke = (kd * halves(jnp.exp(tot - m_a), jnp.exp(tot - m_b))).astype(BF16)
            e["decay"] = jnp.exp(tot)
            if e["rev"]:
                qx = jnp.concatenate([qd[:half] * jnp.exp(m_a - m_mid), zeros], axis=0)
                kx = jnp.concatenate([zeros, kd[half:] * jnp.exp(m_mid - m_b)], axis=0)
            else:
                qx = jnp.concatenate([zeros, qd[half:] * jnp.exp(m_b - m_mid)], axis=0)
                kx = jnp.concatenate([kd[:half] * jnp.exp(m_mid - m_a), zeros], axis=0)
            e["s_in"] = lax.dot_general(qd.astype(BF16), kd.astype(BF16), _NT, preferred_element_type=F32)
            e["s_x"] = lax.dot_general(qx.astype(BF16), kx.astype(BF16), _NT, preferred_element_type=F32)
            e["upd"] = lax.dot_general(e["v"], ke, _TN, preferred_element_type=F32)
        for e in order:
            s = (jnp.where(e["causal"], e["s_in"], 0.0) + e["s_x"]).astype(BF16)
            e["o"] = jnp.dot(s, e["v"], preferred_element_type=F32)
        states = [st_ref[...] for _, st_ref, *_ in scans]
        for idx, e in enumerate(order):
            st = states[idx % 2]
            o = e["o"] + lax.dot_general(e["qe"], st.astype(BF16), _NT, preferred_element_type=F32)
            e["out_ref"][e["sl"], :] = o
            states[idx % 2] = st * e["decay"] + e["upd"]
        for st, (_, st_ref, *_) in zip(states, scans):
            st_ref[...] = st
        return carry

    lax.fori_loop(0, ng, scan, 0)

    rows = _row_tile(seq, 512)

    def finish(i, carry):
        sl = pl.ds(pl.multiple_of(i * rows, rows), rows)
        o = of_ref[sl, :] + ob_ref[sl, :]
        o = o * lax.rsqrt(jnp.mean(o * o, axis=-1, keepdims=True) + EPS)
        o_ref[0, 0, sl, :] = (o * gn_ref[...] * _silu(g_ref[0, 0, sl, :].astype(F32))).astype(o_ref.dtype)
        return carry

    lax.fori_loop(0, seq // rows, finish, 0)


def hgrn_scan(qvg, gates, lb, gnorm):
    b, _, l, _ = qvg.shape
    part = lambda p: pl.BlockSpec((1, 1, l, HG_DK), lambda i, h, p=p: (i, p * HG_HEADS + h, 0, 0))
    return pl.pallas_call(
        functools.partial(_hgrn_kernel, seq=l),
        out_shape=jax.ShapeDtypeStruct((b, HG_HEADS, l, HG_DK), BF16),
        grid=(b, HG_HEADS),
        in_specs=[part(0), part(0), part(1), part(1), part(2),
                  pl.BlockSpec((1, 1, HG_DK), lambda i, h: (h, 0, 0)),
                  pl.BlockSpec((1, HG_DK), lambda i, h: (0, 0))],
        out_specs=pl.BlockSpec((1, 1, l, HG_DK), lambda i, h: (i, h, 0, 0)),
        scratch_shapes=[pltpu.VMEM((l, HG_DK), F32), pltpu.VMEM((l, HG_DK), F32),
                        pltpu.VMEM((HG_DK, HG_DK), F32), pltpu.VMEM((HG_DK, HG_DK), F32)],
        compiler_params=_params("parallel", "parallel"),
        name="hgrn_scan",
    )(qvg, gates, gates, qvg, qvg, lb, gnorm)


def _band_kernel(slope_ref, q_ref, k_ref, v_ref, o_ref, l_ref, bias_ref, *, n, bq, win, dil, pairs, res):
    nq = n // bq
    lane = lax.broadcasted_iota(jnp.int32, (1, LANES), 1)
    head0 = lane < ATT_HD
    qsel = (jnp.where(head0, 1.0, 0.0).astype(BF16), jnp.where(head0, 0.0, 1.0).astype(BF16))
    ones0 = jnp.broadcast_to(jnp.where(head0, 1.0, 0.0).astype(BF16), (win, LANES))
    ones1 = jnp.broadcast_to(jnp.where(head0, 0.0, 1.0).astype(BF16), (win, LANES))
    delta = (lax.broadcasted_iota(jnp.int32, (bq, win), 1) - lax.broadcasted_iota(jnp.int32, (bq, win), 0))

    def make_bias(pi, offset):
        rel = jnp.abs(delta + offset)
        base = jnp.where(rel <= ATT_RADIUS, -(dil * rel).astype(F32), NEG)
        pair = pl.program_id(2) * pairs + pi
        return [slope_ref[2 * pair + h] * base for h in range(2)]

    def scores(pi, ri, q0, k0, bias):
        q = q_ref[0, pi, ri, pl.ds(q0, bq), :]
        k = k_ref[0, pi, ri, pl.ds(k0, win), :]
        ps, ms = [], []
        for h in range(2):
            s = lax.dot_general(q * qsel[h], k, _NT, preferred_element_type=F32) + bias[h]
            m = jnp.max(s, axis=-1, keepdims=True)
            ps.append(jnp.exp2(s - m).astype(BF16))
            ms.append(m)
        return pi, ri, q0, k0, ps, ms

    def finish(pi, ri, q0, k0, ps, ms):
        v = v_ref[0, pi, ri, pl.ds(k0, win), :]
        rhs = jnp.concatenate([
            jnp.concatenate([jnp.where(head0, v, 0), ones0], axis=1),
            jnp.concatenate([jnp.where(head0, 0, v), ones1], axis=1)], axis=0)
        r = jnp.dot(jnp.concatenate(ps, axis=1), rhs, preferred_element_type=F32)
        den = r[:, LANES:]
        o_ref[0, pi, ri, pl.ds(q0, bq), :] = (r[:, :LANES] / den).astype(o_ref.dtype)
        l_ref[0, pi, ri, pl.ds(q0, bq), :] = jnp.where(head0, ms[0], ms[1]) + jnp.log2(den)

    def blocks(todo):
        for i in range(0, len(todo), ATT_PHASE):
            for part in [scores(*t) for t in todo[i:i + ATT_PHASE]]:
                finish(*part)

    planes = [(pi, ri) for pi in range(pairs) for ri in range(res)]

    def edge(q0, k0):
        biases = [make_bias(pi, k0 - q0) for pi in range(pairs)]
        blocks([(pi, ri, q0, k0, biases[pi]) for pi, ri in planes])

    edge(0, 0)
    if nq > 2:
        for pi in range(pairs):
            b0, b1 = make_bias(pi, -ATT_RADIUS)
            bias_ref[pi, 0] = b0
            bias_ref[pi, 1] = b1
        unroll = max(1, ATT_INFLIGHT // len(planes))
        trips, rest = divmod(nq - 2, unroll)

        def interior(first, count):
            todo = []
            for u in range(count):
                q0 = (first + u) * bq
                k0 = q0 - ATT_RADIUS
                if not isinstance(first, int):
                    q0, k0 = pl.multiple_of(q0, bq), pl.multiple_of(k0, ATT_RADIUS)
                todo += [(pi, ri, q0, k0, (bias_ref[pi, 0], bias_ref[pi, 1])) for pi, ri in planes]
            blocks(todo)

        def body(i, carry):
            interior(1 + i * unroll, unroll)
            return carry

        lax.fori_loop(0, trips, body, 0)
        interior(1 + trips * unroll, rest)
    if nq > 1:
        edge(n - bq, n - win)


def band_attention(slopes, qkv):
    b, _, dil, n, _ = qkv.shape
    bq = min(ATT_QBLOCK, n)
    win = min(bq + 2 * ATT_RADIUS, n)
    pairs = max(1, min(N_SLABS, ATT_ROWS_PER_STEP // n))
    res = max(1, min(dil, ATT_ROWS_PER_STEP // (pairs * n)))
    part = lambda c: pl.BlockSpec((1, pairs, res, n, LANES), lambda bi, r, j, c=c: (bi, c * (N_SLABS // pairs) + j, r, 0, 0))
    out = pl.BlockSpec((1, pairs, res, n, LANES), lambda bi, r, j: (bi, j, r, 0, 0))
    return pl.pallas_call(
        functools.partial(_band_kernel, n=n, bq=bq, win=win, dil=dil, pairs=pairs, res=res),
        out_shape=(jax.ShapeDtypeStruct((b, N_SLABS, dil, n, LANES), BF16),
                   jax.ShapeDtypeStruct((b, N_SLABS, dil, n, LANES), F32)),
        grid=(b, dil // res, N_SLABS // pairs),
        in_specs=[pl.BlockSpec(memory_space=pltpu.SMEM), part(0), part(1), part(2)],
        out_specs=(out, out),
        scratch_shapes=[pltpu.VMEM((pairs, 2, bq, win), F32)],
        compiler_params=_params("parallel", "parallel", "parallel"),
        name="band_attention",
    )(slopes, qkv, qkv, qkv)


def _trunk(x, mem, p):
    b, l, d = x.shape
    t = b * l
    kv = kv_projection(mem, p["gains"][:, 5], p["xa_w_kv"])
    for i in range(DEPTH):
        g = p["gains"][i]
        xattn = (kv, i, p["xa_w_q"][i], p["xa_w_o"][i], g[4], g[6])
        x = ffn_block(x.reshape(t, d), g[0], g[1], p["ffn_w_in"], p["ffn_w_out"], (i, 0)).reshape(b, l, d)
        j = i // 2
        if i % 2 == 0:
            qvg, gates = rms_matmul(x, g[2], p["hg_w_in"], (j,),
                                    [([(0, d), (3 * d, 2 * d)], BF16), ([(d, 2 * d)], F32)], "planes")
            mix = [hgrn_scan(qvg.reshape(b, -1, l, LANES), gates.reshape(b, -1, l, LANES),
                             p["lower_bounds"][i], p["hg_gnorm"][j])]
            x = mixer_xattn_block(x, mix, (p["hg_w_out"][j], g[3]), xattn)
        else:
            res = [band_attention(p["slopes"], *rms_matmul(x, g[2], p["att_w_in"], (j,),
                                                           [([(3 * d * gi, 3 * d)], BF16)], "planes", dil))
                   for gi, (_, dil) in enumerate(DIL_PATTERNS)]
            mix = [o for o, _ in res] + [lse for _, lse in res]
            x = mixer_xattn_block(x, mix, (p["att_w_out"][j], g[3]))
            x = mixer_xattn_block(x, xattn_params=xattn)
        x = ffn_block(x.reshape(t, d), g[7], g[8], p["ffn_w_in"], p["ffn_w_out"], (i, 1)).reshape(b, l, d)
    return x


def kernel(x_prompt, x_sample, mem_prompt, mem_sample, norm_gains, ffn_w_in, ffn_w_out, hg_w_in, hg_lb_logits, hg_gnorm, hg_w_out, att_w_in, att_w_out, xa_w_q, xa_w_kv, xa_w_o):
    sm = jax.nn.softmax(hg_lb_logits.astype(F32), axis=0)
    lower_bounds = jnp.maximum(jnp.cumsum(sm, axis=0) - sm[0], 0.0)
    p = {
        "gains": norm_gains.astype(F32).reshape(DEPTH, -1, 1, D_MODEL),
        "ffn_w_in": ffn_w_in.astype(BF16),
        "ffn_w_out": ffn_w_out.astype(BF16),
        "hg_w_in": hg_w_in.astype(BF16),
        "lower_bounds": lower_bounds.reshape(DEPTH, HG_HEADS, 1, HG_DK),
        "hg_gnorm": hg_gnorm.astype(F32).reshape(-1, 1, HG_DK),
        "hg_w_out": hg_w_out.astype(BF16),
        "att_w_in": (att_w_in.reshape(-1, D_MODEL, len(DIL_PATTERNS), 3, D_MODEL)
                     * jnp.array([ATT_HD ** -0.5 * LOG2E, 1.0, 1.0], F32)[:, None]).astype(BF16).reshape(att_w_in.shape),
        "att_w_out": att_w_out.astype(BF16),
        "xa_w_q": xa_w_q.astype(BF16),
        "xa_w_kv": xa_w_kv.astype(BF16),
        "xa_w_o": xa_w_o.astype(BF16),
        "slopes": LOG2E * jnp.exp2(-8.0 * jnp.arange(1, ATT_HEADS + 1, dtype=F32) / ATT_HEADS),
    }
    return (_trunk(x_prompt, mem_prompt, p), _trunk(x_sample, mem_sample, p))
```

```python
import functools

import jax
import jax.numpy as jnp
from jax import lax
from jax.experimental import pallas as pl
from jax.experimental.pallas import tpu as pltpu

F32 = jnp.float32
BF16 = jnp.bfloat16

D_MODEL = 1024
LANES = 128
N_SLABS = D_MODEL // LANES
DEPTH = 4
HG_HEADS = 8
HG_DK = 128
DIL_PATTERNS = ((128, 1), (512, 4), (2048, 16))
ATT_HEADS = 16
ATT_HD = 64
ATT_RADIUS = 64
X_HEADS = 4
X_HD = 256
D_FF = 2816
EPS = 1e-6
NEG = -1e30
LOG2E = 1.4426950408889634

HG_CHUNK = 64
HG_GROUP = 32
FF_CHUNK = 256
MM_TILE_N = 1024
SUB_STRIDE = 4
ATT_QBLOCK = 128
ATT_INFLIGHT = 16
ATT_PHASE = 4
ATT_ROWS_PER_STEP = 8192
V7X_VMEM_BYTES = 64 * 1024 * 1024
VMEM_LIMIT = V7X_VMEM_BYTES * 7 // 8

_NT = (((1,), (1,)), ((), ()))
_TN = (((0,), (0,)), ((), ()))


def _params(*sem):
    return pltpu.CompilerParams(dimension_semantics=sem, vmem_limit_bytes=VMEM_LIMIT)


def _rms(x, g):
    return x * lax.rsqrt(jnp.mean(x * x, axis=-1, keepdims=True) + EPS) * g


def _silu(x):
    return x / (1.0 + jnp.exp(-x))


def _row_tile(t, pref):
    while t % pref:
        pref //= 2
    return pref


def _slabs(ref):
    return jnp.concatenate([ref[0, c] for c in range(ref.shape[1])], axis=-1)


def _rms_matmul_kernel(x_ref, g_ref, w_ref, *refs, layout, dil, splits):
    o_refs, scratch = refs[:len(splits)], refs[len(splits):]
    h = _rms(x_ref[0], g_ref[...]).astype(BF16)
    tn = MM_TILE_N
    spt = tn // LANES
    for o_ref, starts in zip(o_refs, splits):
        for j, c0 in enumerate(starts):
            y = jnp.dot(h, w_ref[:, c0:c0 + tn], preferred_element_type=F32)
            if layout == "rows":
                o_ref[0, :, j * tn:(j + 1) * tn] = y.astype(o_ref.dtype)
            elif dil == 1:
                for c in range(spt):
                    o_ref[0, j * spt + c, 0] = y[:, c * LANES:(c + 1) * LANES].astype(o_ref.dtype)
            else:
                y_ref, *mid = scratch
                rows = y.shape[0] // dil
                d1 = SUB_STRIDE if dil > SUB_STRIDE else 1
                d2 = dil // d1
                for c in range(spt):
                    y_ref[c] = y[:, c * LANES:(c + 1) * LANES]
                    for q in range(d1):
                        if d1 == 1:
                            src = y_ref.at[c]
                        else:
                            mid[0][c, q] = y_ref[c, pl.ds(q, rows * d2, stride=d1), :]
                            src = mid[0].at[c, q]
                        for p in range(d2):
                            o_ref[0, j * spt + c, q + d1 * p] = src[pl.ds(p, rows, stride=d2), :].astype(o_ref.dtype)


def _stacked(w, index):
    lead = len(index)
    return pl.BlockSpec((None,) * lead + w.shape[lead:], lambda *_: (*index, 0, 0), pipeline_mode=pl.Buffered(1))


def rms_matmul(x, g, w, w_index, outputs, layout, dil=1, tm=512):
    b, l, d = x.shape
    tn = MM_TILE_N
    tm = _row_tile(l, tm)
    out_shapes, out_specs, splits = [], [], []
    for ranges, dtype in outputs:
        n = sum(cnt for _, cnt in ranges)
        if layout == "rows":
            out_shapes.append(jax.ShapeDtypeStruct((b, l, n), dtype))
            out_specs.append(pl.BlockSpec((1, tm, n), lambda bi, i: (bi, i, 0)))
        else:
            out_shapes.append(jax.ShapeDtypeStruct((b, n // LANES, dil, l // dil, LANES), dtype))
            out_specs.append(pl.BlockSpec((1, n // LANES, dil, tm // dil, LANES), lambda bi, i: (bi, 0, 0, i, 0)))
        splits.append(tuple(c for c0, cnt in ranges for c in range(c0, c0 + cnt, tn)))
    scratch = []
    if dil > 1:
        scratch.append(pltpu.VMEM((tn // LANES, tm, LANES), F32))
    if dil > SUB_STRIDE:
        scratch.append(pltpu.VMEM((tn // LANES, SUB_STRIDE, tm // SUB_STRIDE, LANES), F32))
    return pl.pallas_call(
        functools.partial(_rms_matmul_kernel, layout=layout, dil=dil, splits=tuple(splits)),
        out_shape=tuple(out_shapes),
        grid=(b, l // tm),
        in_specs=[
            pl.BlockSpec((1, tm, d), lambda bi, i: (bi, i, 0)),
            pl.BlockSpec((1, d), lambda bi, i: (0, 0)),
            _stacked(w, w_index),
        ],
        out_specs=tuple(out_specs),
        scratch_shapes=scratch,
        compiler_params=_params("parallel", "parallel"),
        name="rms_matmul",
    )(x, g, w)


def _ffn_kernel(x_ref, g0_ref, g1_ref, wi_ref, wo_ref, o_ref, acc_ref):
    nf = D_FF // FF_CHUNK
    cols = [slice(c * FF_CHUNK, (c + 1) * FF_CHUNK) for c in range(2 * nf)]
    half = x_ref.shape[0] // 2
    rows = (pl.ds(0, half), pl.ds(half, half))

    def norm_in(r):
        return _rms(x_ref[rows[r], :], g0_ref[...]).astype(BF16)

    def norm_out(r):
        o_ref[rows[r], :] = x_ref[rows[r], :] + 0.5 * _rms(acc_ref[rows[r], :], g1_ref[...])

    h = [norm_in(0), None]
    for r in range(2):
        for c in range(nf):
            gate = jnp.dot(h[r], wi_ref[:, cols[c]], preferred_element_type=F32)
            up = jnp.dot(h[r], wi_ref[:, cols[nf + c]], preferred_element_type=F32)
            act = (_silu(gate) * up).astype(BF16)
            down = jnp.dot(act, wo_ref[cols[c], :], preferred_element_type=F32)
            if c == 0:
                acc_ref[rows[r], :] = down
            else:
                acc_ref[rows[r], :] += down
            if c == nf // 2:
                if r == 0:
                    h[1] = norm_in(1)
                else:
                    norm_out(0)
    norm_out(1)


def ffn_block(x, g0, g1, w_in, w_out, w_index, tm=1024):
    t, d = x.shape
    tm = _row_tile(t, tm)
    row = pl.BlockSpec((tm, d), lambda i: (i, 0))
    gain = pl.BlockSpec((1, d), lambda i: (0, 0))
    return pl.pallas_call(
        _ffn_kernel,
        out_shape=jax.ShapeDtypeStruct((t, d), F32),
        grid=(t // tm,),
        in_specs=[row, gain, gain, _stacked(w_in, w_index), _stacked(w_out, w_index)],
        out_specs=row,
        scratch_shapes=[pltpu.VMEM((tm, d), F32)],
        compiler_params=_params("parallel"),
        name="ffn_block",
    )(x, g0, g1, w_in, w_out)


def _token_rows(ref, buf=None, mid=None):
    dil, sub = ref.shape[2], ref.shape[3]
    if dil == 1:
        return jnp.concatenate([ref[0, c, 0] for c in range(N_SLABS)], axis=-1).astype(F32)
    d1 = SUB_STRIDE if dil > SUB_STRIDE else 1
    d2 = dil // d1
    for c in range(N_SLABS):
        for q in range(d1):
            dst = buf.at[c] if d1 == 1 else mid.at[c, q]
            for p in range(d2):
                dst[pl.ds(p, sub, stride=d2), :] = ref[0, c, q + d1 * p].astype(F32)
            if d1 > 1:
                buf[c, pl.ds(q, sub * d2, stride=d1), :] = mid[c, q]
    return jnp.concatenate([buf[c] for c in range(N_SLABS)], axis=-1)


def _merged_groups(o0_ref, o1_ref, o2_ref, l0_ref, l1_ref, l2_ref, bl1, bl2, bo1, bo2, ml2, mo2):
    l0, l1, l2 = _token_rows(l0_ref), _token_rows(l1_ref, bl1), _token_rows(l2_ref, bl2, ml2)
    m = jnp.maximum(jnp.maximum(l0, l1), l2)
    e0, e1, e2 = jnp.exp2(l0 - m), jnp.exp2(l1 - m), jnp.exp2(l2 - m)
    num = e0 * _token_rows(o0_ref) + e1 * _token_rows(o1_ref, bo1) + e2 * _token_rows(o2_ref, bo2, mo2)
    return (num / (e0 + e1 + e2)).astype(BF16)


def _mixer_xattn_kernel(*refs, n_mix, xattn):
    mix_refs, refs = refs[:n_mix], refs[n_mix:]
    if n_mix:
        (wm_ref, gm_ref), refs = refs[:2], refs[2:]
    x_ref, refs = refs[0], refs[1:]
    if xattn:
        (k_ref, v_ref, wq_ref, wo_ref, gq_ref, go_ref), refs = refs[:6], refs[6:]
    o_ref, *scratch = refs
    x = x_ref[0]
    if n_mix:
        mixed = _slabs(mix_refs[0]) if n_mix == 1 else _merged_groups(*mix_refs, *scratch)
        x = x + _rms(jnp.dot(mixed, wm_ref[...], preferred_element_type=F32), gm_ref[...])
    if not xattn:
        o_ref[0] = x
        return
    q = jnp.dot(_rms(x, gq_ref[...]).astype(BF16), wq_ref[...], preferred_element_type=F32).astype(BF16)
    k = k_ref[0]
    v = v_ref[0]
    cols = [slice(hd * X_HD, (hd + 1) * X_HD) for hd in range(X_HEADS)]
    scores = [lax.dot_general(q[:, sl], k[:, sl], _NT, preferred_element_type=F32) * (X_HD ** -0.5) for sl in cols]
    probs = [jnp.exp(s - jnp.max(s, axis=-1, keepdims=True)) for s in scores]
    heads = [jnp.dot(p.astype(BF16), v[:, sl], preferred_element_type=F32) / jnp.sum(p, axis=-1, keepdims=True)
             for p, sl in zip(probs, cols)]
    c = jnp.dot(jnp.concatenate(heads, axis=-1).astype(BF16), wo_ref[...], preferred_element_type=F32)
    o_ref[0] = x + _rms(c, go_ref[...])


def mixer_xattn_block(x, mix=(), mix_params=(), xattn_params=()):
    b, l, d = x.shape
    merge = len(mix) > 1
    tm = _row_tile(l, 512)
    row = pl.BlockSpec((1, tm, d), lambda bi, i: (bi, i, 0))
    full = pl.BlockSpec((d, d), lambda bi, i: (0, 0), pipeline_mode=pl.Buffered(1))
    gain = pl.BlockSpec((1, d), lambda bi, i: (0, 0))
    specs, args, scratch = [], [], []
    if merge:
        specs += [pl.BlockSpec((1, N_SLABS, a.shape[2], tm // a.shape[2], LANES), lambda bi, i: (bi, 0, 0, i, 0))
                  for a in mix]
        scratch = ([pltpu.VMEM((N_SLABS, tm, LANES), F32)] * 4
                   + [pltpu.VMEM((N_SLABS, SUB_STRIDE, tm // SUB_STRIDE, LANES), F32)] * 2)
    elif mix:
        specs += [pl.BlockSpec((1, N_SLABS, tm, LANES), lambda bi, i: (bi, 0, i, 0))]
    if mix:
        specs += [full, gain]
        args += [*mix, *mix_params]
    specs.append(row)
    args.append(x)
    if xattn_params:
        kv, layer, w_q, w_o, g_q, g_o = xattn_params
        m = kv.shape[2]
        specs += [pl.BlockSpec((None, 1, m, d), lambda bi, i: (layer, bi, 0, 0)),
                  pl.BlockSpec((None, 1, m, d), lambda bi, i: (layer, bi, 0, 1)),
                  full, full, gain, gain]
        args += [kv, kv, w_q, w_o, g_q, g_o]
    return pl.pallas_call(
        functools.partial(_mixer_xattn_kernel, n_mix=len(mix), xattn=bool(xattn_params)),
        out_shape=jax.ShapeDtypeStruct((b, l, d), F32),
        grid=(b, l // tm),
        in_specs=specs,
        out_specs=row,
        scratch_shapes=scratch,
        compiler_params=_params("parallel", "parallel"),
        name="mixer_xattn_block",
    )(*args)


def kv_projection(mem, gains, w):
    b, m, d = mem.shape
    depth, _, n = w.shape
    return pl.pallas_call(
        functools.partial(_rms_matmul_kernel, layout="rows", dil=1, splits=(tuple(range(0, n, MM_TILE_N)),)),
        out_shape=jax.ShapeDtypeStruct((depth, b, m, n), BF16),
        grid=(depth, b),
        in_specs=[
            pl.BlockSpec((1, m, d), lambda li, bi: (bi, 0, 0)),
            pl.BlockSpec((None, 1, d), lambda li, bi: (li, 0, 0)),
            pl.BlockSpec((None, d, n), lambda li, bi: (li, 0, 0)),
        ],
        out_specs=pl.BlockSpec((None, 1, m, n), lambda li, bi: (li, bi, 0, 0)),
        compiler_params=_params("parallel", "parallel"),
        name="kv_projection",
    )(mem, gains, w)


def _hgrn_kernel(q_ref, zf_ref, zb_ref, v_ref, g_ref, lb_ref, gn_ref, o_ref, of_ref, ob_ref, stf_ref, stb_ref, *, seq):
    c_len = HG_CHUNK
    nc = seq // c_len
    half = c_len // 2
    lb = lb_ref[0]
    log_lb = jnp.log(lb)
    log_1m_lb = jnp.log1p(-lb)
    row = lax.broadcasted_iota(jnp.int32, (c_len, c_len), 0)
    col = lax.broadcasted_iota(jnp.int32, (c_len, c_len), 1)
    same_half = (row < half) == (col < half)
    lower = (col <= row) & same_half
    upper = (col >= row) & same_half
    lower_sum = (col <= row).astype(BF16)
    upper_sum = (col >= row).astype(BF16)

    def halves(first, second):
        return jnp.concatenate([jnp.broadcast_to(first, (half, HG_DK)), jnp.broadcast_to(second, (half, HG_DK))], axis=0)

    def gate(z):
        e = jnp.exp(-jnp.abs(z))
        log_sig = jnp.minimum(z, 0.0) - jnp.log(1.0 + e)
        t = log_1m_lb + log_sig
        logf = jnp.maximum(log_lb, t) + jnp.log(1.0 + jnp.exp(-jnp.abs(log_lb - t)))
        sig_neg = jnp.where(z >= 0.0, e, 1.0) / (1.0 + e)
        return logf, (1.0 - lb) * sig_neg

    grp = _row_tile(nc, HG_GROUP)
    ng = nc // grp
    quarter = half // 2
    scans = ((zf_ref, stf_ref, of_ref, lower_sum, lower, (quarter - 1, half - 1, half + quarter - 1, c_len - 1), False),
             (zb_ref, stb_ref, ob_ref, upper_sum, upper, (quarter, half, half + quarter, 0), True))

    stf_ref[...] = jnp.zeros_like(stf_ref)
    stb_ref[...] = jnp.zeros_like(stb_ref)

    def scan(i, carry):
        steps = []
        for z_ref, st_ref, out_ref, sum_mat, causal, ref_rows, rev in scans:
            g0 = (ng - 1 - i) if rev else i
            sls = [pl.ds(pl.multiple_of((g0 * grp + j) * c_len, c_len), c_len) for j in range(grp)]
            gates = [gate(z_ref[0, 0, sl, :]) for sl in sls]
            logf = jnp.concatenate([lf for lf, _ in gates], axis=1)
            hi = logf.astype(BF16)
            lo = (logf - hi.astype(F32)).astype(BF16)
            cums = jnp.dot(jnp.concatenate([sum_mat, sum_mat], axis=1), jnp.concatenate([hi, lo], axis=0),
                           preferred_element_type=F32)
            for j in (reversed(range(grp)) if rev else range(grp)):
                steps.append(dict(sl=sls[j], k=gates[j][1], cum=cums[:, j * HG_DK:(j + 1) * HG_DK], causal=causal,
                                  ref_rows=ref_rows, rev=rev, st_ref=st_ref, out_ref=out_ref))
        order = [steps[d * grp + j] for j in range(grp) for d in range(2)]
        zeros = jnp.zeros((half, HG_DK), F32)
        for e in order:
            cum = e["cum"]
            m_a, m_mid, m_b, tot = (cum[r:r + 1, :] for r in e["ref_rows"])
            q = _silu(q_ref[0, 0, e["sl"], :].astype(F32))
            e["v"] = v_ref[0, 0, e["sl"], :]
            ref = halves(m_a, m_b)
            qd = q * jnp.exp(cum - ref)
            kd = e["k"] * jnp.exp(ref - cum)
            e["qe"] = (qd * halves(jnp.exp(m_a), jnp.exp(m_b))).astype(BF16)
            ke = (kd * halves(jnp.exp(tot - m_a), jnp.exp(tot - m_b))).astype(BF16)
            e["decay"] = jnp.exp(tot)
            if e["rev"]:
                qx = jnp.concatenate([qd[:half] * jnp.exp(m_a - m_mid), zeros], axis=0)
                kx = jnp.concatenate([zeros, kd[half:] * jnp.exp(m_mid - m_b)], axis=0)
            else:
                qx = jnp.concatenate([zeros, qd[half:] * jnp.exp(m_b - m_mid)], axis=0)
                kx = jnp.concatenate([kd[:half] * jnp.exp(m_mid - m_a), zeros], axis=0)
            e["s_in"] = lax.dot_general(qd.astype(BF16), kd.astype(BF16), _NT, preferred_element_type=F32)
            e["s_x"] = lax.dot_general(qx.astype(BF16), kx.astype(BF16), _NT, preferred_element_type=F32)
            e["upd"] = lax.dot_general(e["v"], ke, _TN, preferred_element_type=F32)
        for e in order:
            s = (jnp.where(e["causal"], e["s_in"], 0.0) + e["s_x"]).astype(BF16)
            e["o"] = jnp.dot(s, e["v"], preferred_element_type=F32)
        states = [st_ref[...] for _, st_ref, *_ in scans]
        for idx, e in enumerate(order):
            st = states[idx % 2]
            o = e["o"] + lax.dot_general(e["qe"], st.astype(BF16), _NT, preferred_element_type=F32)
            e["out_ref"][e["sl"], :] = o
            states[idx % 2] = st * e["decay"] + e["upd"]
        for st, (_, st_ref, *_) in zip(states, scans):
            st_ref[...] = st
        return carry

    lax.fori_loop(0, ng, scan, 0)

    rows = _row_tile(seq, 512)

    def finish(i, carry):
        sl = pl.ds(pl.multiple_of(i * rows, rows), rows)
        o = of_ref[sl, :] + ob_ref[sl, :]
        o = o * lax.rsqrt(jnp.mean(o * o, axis=-1, keepdims=True) + EPS)
        o_ref[0, 0, sl, :] = (o * gn_ref[...] * _silu(g_ref[0, 0, sl, :].astype(F32))).astype(o_ref.dtype)
        return carry

    lax.fori_loop(0, seq // rows, finish, 0)


def hgrn_scan(qvg, gates, lb, gnorm):
    b, _, l, _ = qvg.shape
    part = lambda p: pl.BlockSpec((1, 1, l, HG_DK), lambda i, h, p=p: (i, p * HG_HEADS + h, 0, 0))
    return pl.pallas_call(
        functools.partial(_hgrn_kernel, seq=l),
        out_shape=jax.ShapeDtypeStruct((b, HG_HEADS, l, HG_DK), BF16),
        grid=(b, HG_HEADS),
        in_specs=[part(0), part(0), part(1), part(1), part(2),
                  pl.BlockSpec((1, 1, HG_DK), lambda i, h: (h, 0, 0)),
                  pl.BlockSpec((1, HG_DK), lambda i, h: (0, 0))],
        out_specs=pl.BlockSpec((1, 1, l, HG_DK), lambda i, h: (i, h, 0, 0)),
        scratch_shapes=[pltpu.VMEM((l, HG_DK), F32), pltpu.VMEM((l, HG_DK), F32),
                        pltpu.VMEM((HG_DK, HG_DK), F32), pltpu.VMEM((HG_DK, HG_DK), F32)],
        compiler_params=_params("parallel", "parallel"),
        name="hgrn_scan",
    )(qvg, gates, gates, qvg, qvg, lb, gnorm)


def _band_kernel(slope_ref, q_ref, k_ref, v_ref, o_ref, l_ref, bias_ref, *, n, bq, win, dil, pairs, res):
    nq = n // bq
    lane = lax.broadcasted_iota(jnp.int32, (1, LANES), 1)
    head0 = lane < ATT_HD
    qsel = (jnp.where(head0, 1.0, 0.0).astype(BF16), jnp.where(head0, 0.0, 1.0).astype(BF16))
    ones0 = jnp.broadcast_to(jnp.where(head0, 1.0, 0.0).astype(BF16), (win, LANES))
    ones1 = jnp.broadcast_to(jnp.where(head0, 0.0, 1.0).astype(BF16), (win, LANES))
    delta = (lax.broadcasted_iota(jnp.int32, (bq, win), 1) - lax.broadcasted_iota(jnp.int32, (bq, win), 0))

    def make_bias(pi, offset):
        rel = jnp.abs(delta + offset)
        base = jnp.where(rel <= ATT_RADIUS, -(dil * rel).astype(F32), NEG)
        pair = pl.program_id(2) * pairs + pi
        return [slope_ref[2 * pair + h] * base for h in range(2)]

    def scores(pi, ri, q0, k0, bias):
        q = q_ref[0, pi, ri, pl.ds(q0, bq), :]
        k = k_ref[0, pi, ri, pl.ds(k0, win), :]
        ps, ms = [], []
        for h in range(2):
            s = lax.dot_general(q * qsel[h], k, _NT, preferred_element_type=F32) + bias[h]
            m = jnp.max(s, axis=-1, keepdims=True)
            ps.append(jnp.exp2(s - m).astype(BF16))
            ms.append(m)
        return pi, ri, q0, k0, ps, ms

    def finish(pi, ri, q0, k0, ps, ms):
        v = v_ref[0, pi, ri, pl.ds(k0, win), :]
        rhs = jnp.concatenate([
            jnp.concatenate([jnp.where(head0, v, 0), ones0], axis=1),
            jnp.concatenate([jnp.where(head0, 0, v), ones1], axis=1)], axis=0)
        r = jnp.dot(jnp.concatenate(ps, axis=1), rhs, preferred_element_type=F32)
        den = r[:, LANES:]
        o_ref[0, pi, ri, pl.ds(q0, bq), :] = (r[:, :LANES] / den).astype(o_ref.dtype)
        l_ref[0, pi, ri, pl.ds(q0, bq), :] = jnp.where(head0, ms[0], ms[1]) + jnp.log2(den)

    def blocks(todo):
        for i in range(0, len(todo), ATT_PHASE):
            for part in [scores(*t) for t in todo[i:i + ATT_PHASE]]:
                finish(*part)

    planes = [(pi, ri) for pi in range(pairs) for ri in range(res)]

    def edge(q0, k0):
        biases = [make_bias(pi, k0 - q0) for pi in range(pairs)]
        blocks([(pi, ri, q0, k0, biases[pi]) for pi, ri in planes])

    edge(0, 0)
    if nq > 2:
        for pi in range(pairs):
            b0, b1 = make_bias(pi, -ATT_RADIUS)
            bias_ref[pi, 0] = b0
            bias_ref[pi, 1] = b1
        unroll = max(1, ATT_INFLIGHT // len(planes))
        trips, rest = divmod(nq - 2, unroll)

        def interior(first, count):
            todo = []
            for u in range(count):
                q0 = (first + u) * bq
                k0 = q0 - ATT_RADIUS
                if not isinstance(first, int):
                    q0, k0 = pl.multiple_of(q0, bq), pl.multiple_of(k0, ATT_RADIUS)
                todo += [(pi, ri, q0, k0, (bias_ref[pi, 0], bias_ref[pi, 1])) for pi, ri in planes]
            blocks(todo)

        def body(i, carry):
            interior(1 + i * unroll, unroll)
            return carry

        lax.fori_loop(0, trips, body, 0)
        interior(1 + trips * unroll, rest)
    if nq > 1:
        edge(n - bq, n - win)


def band_attention(slopes, qkv):
    b, _, dil, n, _ = qkv.shape
    bq = min(ATT_QBLOCK, n)
    win = min(bq + 2 * ATT_RADIUS, n)
    pairs = max(1, min(N_SLABS, ATT_ROWS_PER_STEP // n))
    res = max(1, min(dil, ATT_ROWS_PER_STEP // (pairs * n)))
    part = lambda c: pl.BlockSpec((1, pairs, res, n, LANES), lambda bi, r, j, c=c: (bi, c * (N_SLABS // pairs) + j, r, 0, 0))
    out = pl.BlockSpec((1, pairs, res, n, LANES), lambda bi, r, j: (bi, j, r, 0, 0))
    return pl.pallas_call(
        functools.partial(_band_kernel, n=n, bq=bq, win=win, dil=dil, pairs=pairs, res=res),
        out_shape=(jax.ShapeDtypeStruct((b, N_SLABS, dil, n, LANES), BF16),
                   jax.ShapeDtypeStruct((b, N_SLABS, dil, n, LANES), F32)),
        grid=(b, dil // res, N_SLABS // pairs),
        in_specs=[pl.BlockSpec(memory_space=pltpu.SMEM), part(0), part(1), part(2)],
        out_specs=(out, out),
        scratch_shapes=[pltpu.VMEM((pairs, 2, bq, win), F32)],
        compiler_params=_params("parallel", "parallel", "parallel"),
        name="band_attention",
    )(slopes, qkv, qkv, qkv)


def _trunk(x, mem, p):
    b, l, d = x.shape
    t = b * l
    kv = kv_projection(mem, p["gains"][:, 5], p["xa_w_kv"])
    for i in range(DEPTH):
        g = p["gains"][i]
        xattn = (kv, i, p["xa_w_q"][i], p["xa_w_o"][i], g[4], g[6])
        x = ffn_block(x.reshape(t, d), g[0], g[1], p["ffn_w_in"], p["ffn_w_out"], (i, 0)).reshape(b, l, d)
        j = i // 2
        if i % 2 == 0:
            qvg, gates = rms_matmul(x, g[2], p["hg_w_in"], (j,),
                                    [([(0, d), (3 * d, 2 * d)], BF16), ([(d, 2 * d)], F32)], "planes")
            mix = [hgrn_scan(qvg.reshape(b, -1, l, LANES), gates.reshape(b, -1, l, LANES),
                             p["lower_bounds"][i], p["hg_gnorm"][j])]
            x = mixer_xattn_block(x, mix, (p["hg_w_out"][j], g[3]), xattn)
        else:
            res = [band_attention(p["slopes"], *rms_matmul(x, g[2], p["att_w_in"], (j,),
                                                           [([(3 * d * gi, 3 * d)], BF16)], "planes", dil))
                   for gi, (_, dil) in enumerate(DIL_PATTERNS)]
            mix = [o for o, _ in res] + [lse for _, lse in res]
            x = mixer_xattn_block(x, mix, (p["att_w_out"][j], g[3]))
            x = mixer_xattn_block(x, xattn_params=xattn)
        x = ffn_block(x.reshape(t, d), g[7], g[8], p["ffn_w_in"], p["ffn_w_out"], (i, 1)).reshape(b, l, d)
    return x


def kernel(x_prompt, x_sample, mem_prompt, mem_sample, norm_gains, ffn_w_in, ffn_w_out, hg_w_in, hg_lb_logits, hg_gnorm, hg_w_out, att_w_in, att_w_out, xa_w_q, xa_w_kv, xa_w_o):
    sm = jax.nn.softmax(hg_lb_logits.astype(F32), axis=0)
    lower_bounds = jnp.maximum(jnp.cumsum(sm, axis=0) - sm[0], 0.0)
    p = {
        "gains": norm_gains.astype(F32).reshape(DEPTH, -1, 1, D_MODEL),
        "ffn_w_in": ffn_w_in.astype(BF16),
        "ffn_w_out": ffn_w_out.astype(BF16),
        "hg_w_in": hg_w_in.astype(BF16),
        "lower_bounds": lower_bounds.reshape(DEPTH, HG_HEADS, 1, HG_DK),
        "hg_gnorm": hg_gnorm.astype(F32).reshape(-1, 1, HG_DK),
        "hg_w_out": hg_w_out.astype(BF16),
        "att_w_in": (att_w_in.reshape(-1, D_MODEL, len(DIL_PATTERNS), 3, D_MODEL)
                     * jnp.array([ATT_HD ** -0.5 * LOG2E, 1.0, 1.0], F32)[:, None]).astype(BF16).reshape(att_w_in.shape),
        "att_w_out": att_w_out.astype(BF16),
        "xa_w_q": xa_w_q.astype(BF16),
        "xa_w_kv": xa_w_kv.astype(BF16),
        "xa_w_o": xa_w_o.astype(BF16),
        "slopes": LOG2E * jnp.exp2(-8.0 * jnp.arange(1, ATT_HEADS + 1, dtype=F32) / ATT_HEADS),
    }
    return (_trunk(x_prompt, mem_prompt, p), _trunk(x_sample, mem_sample, p))
```
